```python
import jax, jax.numpy as jnp
from jax import lax
import numpy as np

D_MODEL = 1024
BATCH = 8
SEQ = 2048
DEPTH = 4

HEAD_DIM = 64
N_DSWA_HEADS = 8
N_RET_HEADS = 8
DSWA_WIDTH = N_DSWA_HEADS * HEAD_DIM
RET_WIDTH = N_RET_HEADS * HEAD_DIM
HYB_WIDTHS = (DSWA_WIDTH, DSWA_WIDTH, DSWA_WIDTH, RET_WIDTH, RET_WIDTH, RET_WIDTH, RET_WIDTH)
HYB_IN_WIDTH = sum(HYB_WIDTHS)
HYB_SPLITS = tuple(int(s) for s in np.cumsum(HYB_WIDTHS)[:-1])
HYB_MIX_WIDTH = DSWA_WIDTH + RET_WIDTH
DSWA_BRANCHES = ((128, 1), (512, 4), (2048, 16))
DSWA_BLOCK = 128
ROPE_THETA = 500000.0
ROPE_DIMS = HEAD_DIM // 4
RET_CHUNK = 128
RET_ROPE_THETA = 10000.0
GMLP_WIDTH = D_MODEL
GMLP_CHUNK = 128
GMLP_GROUPS = 8
GMLP_GROUP_DIM = GMLP_WIDTH // GMLP_GROUPS
D_FF = 2816
N_EVEN = (DEPTH + 1) // 2
N_ODD = DEPTH // 2
EPS = 1e-6
NEG_INF = -1e30

kernel_name = 'hybrid_dilated_retention_gmlp_macaron'


def _rmsnorm(x, g):
    xf = x.astype(jnp.float32)
    y = xf * lax.rsqrt(jnp.mean(xf * xf, axis=-1, keepdims=True) + EPS)
    return (y * g.astype(jnp.float32)).astype(x.dtype)


def _layernorm(x, g, b):
    xf = x.astype(jnp.float32)
    mu = jnp.mean(xf, axis=-1, keepdims=True)
    var = jnp.mean(jnp.square(xf - mu), axis=-1, keepdims=True)
    y = (xf - mu) * lax.rsqrt(var + EPS)
    return (y * g.astype(jnp.float32) + b.astype(jnp.float32)).astype(x.dtype)


def _rope(x, rot_dims, theta):
    t = x.shape[2]
    half = rot_dims // 2
    inv = theta ** (-(jnp.arange(half, dtype=jnp.float32) * 2.0 / rot_dims))
    ang = jnp.arange(t, dtype=jnp.float32)[:, None] * inv[None, :]
    cos, sin = jnp.cos(ang), jnp.sin(ang)
    xr = x[..., :rot_dims].astype(jnp.float32)
    x1, x2 = xr[..., :half], xr[..., half:]
    rot = jnp.concatenate([x1 * cos - x2 * sin, x2 * cos + x1 * sin], axis=-1).astype(x.dtype)
    return jnp.concatenate([rot, x[..., rot_dims:]], axis=-1)


def _swiglu(x, w_gate, w_up, w_down):
    return (jax.nn.silu(x @ w_gate) * (x @ w_up)) @ w_down


def _heads(z, n):
    b, t, _ = z.shape
    return z.reshape(b, t, n, HEAD_DIM).transpose(0, 2, 1, 3)


def _merge(z):
    b, h, t, e = z.shape
    return z.transpose(0, 2, 1, 3).reshape(b, t, h * e)


def _dilated_branch(q, k, v, window, dilation):
    b, h, t, e = q.shape
    length = t // dilation
    band = window // dilation
    n_blk = -(-length // DSWA_BLOCK)
    pad = n_blk * DSWA_BLOCK - length
    n_prev = -(-band // DSWA_BLOCK)
    n_keys = (n_prev + 1) * DSWA_BLOCK

    def to_blocks(z):
        z = z.reshape(b, h, length, dilation, e).transpose(0, 1, 3, 2, 4)
        z = jnp.pad(z, ((0, 0), (0, 0), (0, 0), (0, pad), (0, 0)))
        return z.reshape(b, h, dilation, n_blk, DSWA_BLOCK, e)

    def with_prev(z):
        parts = [jnp.pad(z, ((0, 0), (0, 0), (0, 0), (p, 0), (0, 0), (0, 0)))[:, :, :, :n_blk]
                 for p in range(n_prev, 0, -1)]
        return jnp.concatenate(parts + [z], axis=4)

    qb = to_blocks(q)
    kb = with_prev(to_blocks(k))
    vb = with_prev(to_blocks(v))
    s = jnp.einsum('bhrnqe,bhrnke->bhrnqk', qb, kb).astype(jnp.float32)
    qi = jnp.arange(DSWA_BLOCK)[:, None]
    kj = jnp.arange(n_keys)[None, :]
    dist = n_prev * DSWA_BLOCK + qi - kj
    in_band = (dist >= 0) & (dist <= band)
    key_pos = (jnp.arange(n_blk)[:, None] - n_prev) * DSWA_BLOCK + jnp.arange(n_keys)[None, :]
    mask = in_band[None] & (key_pos >= 0)[:, None, :]
    s = jnp.where(mask, s, NEG_INF)
    lse = jax.nn.logsumexp(s, axis=-1)
    p = jnp.exp(s - lse[..., None]).astype(v.dtype)
    o = jnp.einsum('bhrnqk,bhrnke->bhrnqe', p, vb)
    o = o.reshape(b, h, dilation, n_blk * DSWA_BLOCK, e)[:, :, :, :length]
    o = o.transpose(0, 1, 3, 2, 4).reshape(b, h, t, e)
    lse = lse.reshape(b, h, dilation, n_blk * DSWA_BLOCK)[..., :length]
    lse = lse.transpose(0, 1, 3, 2).reshape(b, h, t)
    return o, lse


def _dilated_attention(q, k, v):
    outs, lses = [], []
    for window, dilation in DSWA_BRANCHES:
        o, l = _dilated_branch(q, k, v, window, dilation)
        outs.append(o)
        lses.append(l)
    wts = jax.nn.softmax(jnp.stack(lses, axis=0), axis=0)
    o = jnp.sum(wts[..., None] * jnp.stack(outs, axis=0).astype(jnp.float32), axis=0)
    return o.astype(q.dtype)


def _retention(q, k, v):
    b, h, t, e = q.shape
    c = RET_CHUNK
    nc = t // c
    log_g = jnp.log(1.0 - jnp.exp2(-5.0 - jnp.arange(h, dtype=jnp.float32)))
    idx = jnp.arange(c, dtype=jnp.float32)
    diff = idx[:, None] - idx[None, :]
    decay = jnp.where(diff >= 0, jnp.exp(log_g[:, None, None] * jnp.maximum(diff, 0.0)), 0.0)
    qc = q.reshape(b, h, nc, c, e)
    kc = k.reshape(b, h, nc, c, e)
    vc = v.reshape(b, h, nc, c, e)
    scores = jnp.einsum('bhncd,bhnmd->bhncm', qc, kc) * decay[None, :, None]
    inner = jnp.einsum('bhncm,bhnme->bhnce', scores.astype(v.dtype), vc)
    zeta = jnp.exp(log_g[:, None] * (c - 1.0 - idx)[None, :])
    kv = jnp.einsum('bhnmd,bhnme->bhnde', kc * zeta[None, :, None, :, None].astype(k.dtype), vc)
    chunk_decay = jnp.exp(log_g * c).astype(kv.dtype)[None, :, None, None]

    def step(state, kv_n):
        return state * chunk_decay + kv_n, state

    _, states = lax.scan(step, jnp.zeros_like(kv[:, :, 0]), jnp.moveaxis(kv, 2, 0))
    states = jnp.moveaxis(states, 0, 2)
    xi = jnp.exp(log_g[:, None] * (idx + 1.0)[None, :])
    cross = jnp.einsum('bhncd,bhnde->bhnce', qc, states) * xi[None, :, None, :, None].astype(q.dtype)
    out = (inner + cross).reshape(b, h, t, e).astype(jnp.float32)
    out = out * lax.rsqrt(jnp.mean(out * out, axis=-1, keepdims=True) + EPS)
    return out.astype(q.dtype)


def _hybrid_mixer(hx, w_in, w_out):
    proj = hx @ w_in
    qa, ka, va, qr, kr, vr, gr = jnp.split(proj, HYB_SPLITS, axis=-1)
    qa = _rope(_heads(qa, N_DSWA_HEADS), ROPE_DIMS, ROPE_THETA) * (HEAD_DIM ** -0.5)
    ka = _rope(_heads(ka, N_DSWA_HEADS), ROPE_DIMS, ROPE_THETA)
    a = _merge(_dilated_attention(qa, ka, _heads(va, N_DSWA_HEADS)))
    qr = _rope(_heads(qr, N_RET_HEADS), HEAD_DIM, RET_ROPE_THETA)
    kr = _rope(_heads(kr, N_RET_HEADS), HEAD_DIM, RET_ROPE_THETA) * (HEAD_DIM ** -0.5)
    r = jax.nn.silu(gr) * _merge(_retention(qr, kr, _heads(vr, N_RET_HEADS)))
    return jnp.concatenate([a, r], axis=-1) @ w_out


def _gmlp_mixer(hx, w_in, ln_g, ln_b, w_s, b_s, w_out):
    z = jax.nn.gelu(hx @ w_in, approximate=False)
    u, v = jnp.split(z, 2, axis=-1)
    v = _layernorm(v, ln_g, ln_b)
    b, t, _ = v.shape
    nc = t // GMLP_CHUNK
    v = v.reshape(b, nc, GMLP_CHUNK, GMLP_GROUPS, GMLP_GROUP_DIM)
    causal = jnp.tril(jnp.ones((GMLP_CHUNK, GMLP_CHUNK), dtype=bool))
    w = jnp.where(causal[None], w_s, 0)
    s = jnp.einsum('gij,bnjge->bnige', w, v) + b_s.T[None, None, :, :, None]
    return (u * s.reshape(b, t, GMLP_WIDTH)) @ w_out


def setup_inputs(seed: int = 0) -> dict:
    key = jax.random.key(seed)
    ks = jax.random.split(key, 16)
    f32 = jnp.float32
    nrm = lambda k, shape, scale: jax.random.normal(k, shape, f32) * scale
    return {
        'x': nrm(ks[0], (BATCH, SEQ, D_MODEL), 1.0),
        'norm_g': 1.0 + nrm(ks[1], (DEPTH, 6, D_MODEL), 0.05),
        'ffn_w_gate': nrm(ks[2], (DEPTH, 2, D_MODEL, D_FF), D_MODEL ** -0.5),
        'ffn_w_up': nrm(ks[3], (DEPTH, 2, D_MODEL, D_FF), D_MODEL ** -0.5),
        'ffn_w_down': nrm(ks[4], (DEPTH, 2, D_FF, D_MODEL), D_FF ** -0.5),
        'hyb_w_in': nrm(ks[5], (N_EVEN, D_MODEL, HYB_IN_WIDTH), D_MODEL ** -0.5),
        'hyb_w_out': nrm(ks[6], (N_EVEN, HYB_MIX_WIDTH, D_MODEL), HYB_MIX_WIDTH ** -0.5),
        'gmlp_w_in': nrm(ks[7], (N_ODD, D_MODEL, 2 * GMLP_WIDTH), D_MODEL ** -0.5),
        'gmlp_ln_g': 1.0 + nrm(ks[8], (N_ODD, GMLP_WIDTH), 0.05),
        'gmlp_ln_b': nrm(ks[9], (N_ODD, GMLP_WIDTH), 0.02),
        'gmlp_w_s': nrm(ks[10], (N_ODD, GMLP_GROUPS, GMLP_CHUNK, GMLP_CHUNK), GMLP_CHUNK ** -0.5),
        'gmlp_b_s': 1.0 + nrm(ks[11], (N_ODD, GMLP_GROUPS, GMLP_CHUNK), 0.02),
        'gmlp_w_out': nrm(ks[12], (N_ODD, GMLP_WIDTH, D_MODEL), GMLP_WIDTH ** -0.5),
    }


def reference(x, norm_g, ffn_w_gate, ffn_w_up, ffn_w_down, hyb_w_in, hyb_w_out,
              gmlp_w_in, gmlp_ln_g, gmlp_ln_b, gmlp_w_s, gmlp_b_s, gmlp_w_out):
    for layer in range(DEPTH):
        g = norm_g[layer]
        f = _swiglu(_rmsnorm(x, g[0]), ffn_w_gate[layer, 0], ffn_w_up[layer, 0], ffn_w_down[layer, 0])
        x = x + 0.5 * _rmsnorm(f, g[1])
        hx = _rmsnorm(x, g[2])
        j = layer // 2
        if layer % 2 == 0:
            m = _hybrid_mixer(hx, hyb_w_in[j], hyb_w_out[j])
        else:
            m = _gmlp_mixer(hx, gmlp_w_in[j], gmlp_ln_g[j], gmlp_ln_b[j],
                            gmlp_w_s[j], gmlp_b_s[j], gmlp_w_out[j])
        x = x + _rmsnorm(m, g[3])
        f = _swiglu(_rmsnorm(x, g[4]), ffn_w_gate[layer, 1], ffn_w_up[layer, 1], ffn_w_down[layer, 1])
        x = x + 0.5 * _rmsnorm(f, g[5])
    return x
```

```python
import functools

import jax
import jax.numpy as jnp
import numpy as np
from jax import lax
from jax.experimental import pallas as pl
from jax.experimental.pallas import tpu as pltpu

F32 = jnp.float32
BF16 = jnp.bfloat16

D_MODEL = 1024
HEAD_DIM = 64
PAIR = 2 * HEAD_DIM
N_PAIRS = 4
MIX_W = 512
BLK = 128
DIL = 16
D_FF = 2816
FF_CHUNK = 256
GMLP_GROUPS = 8
ROPE_THETA = 500000.0
ROPE_DIMS = HEAD_DIM // 4
RET_ROPE_THETA = 10000.0
EPS = 1e-6
NEG_INF = -1e30

FFN_ROWS = 512
GMLP_ROWS = 256
HYB_RES = 4
MAIN_ROWS = 256
VMEM_LIMIT = 56 * 1024 * 1024


def _dot(a, b):
    return jnp.dot(a, b, preferred_element_type=F32)


def _dot_nt(a, b):
    return lax.dot_general(a, b, (((1,), (1,)), ((), ())), preferred_element_type=F32)


def _rms(x, g):
    y = x * lax.rsqrt(jnp.mean(x * x, axis=-1, keepdims=True) + EPS)
    return y * g


def _gelu(x):
    return 0.5 * x * (1.0 + lax.erf(x * np.float32(np.sqrt(0.5))))


def _params(n_axes):
    return pltpu.CompilerParams(
        dimension_semantics=("arbitrary",) * n_axes, vmem_limit_bytes=VMEM_LIMIT)


def _ffn_kernel(x_ref, g_ref, wg_ref, wu_ref, wd_ref, o_ref):
    x = x_ref[...]
    h = _rms(x, g_ref[0:1, :]).astype(BF16)
    acc = jnp.zeros(x.shape, F32)
    for c in range(D_FF // FF_CHUNK):
        sl = slice(c * FF_CHUNK, (c + 1) * FF_CHUNK)
        gate = _dot(h, wg_ref[:, sl])
        up = _dot(h, wu_ref[:, sl])
        act = (gate * jax.nn.sigmoid(gate)) * up
        acc = acc + _dot(act.astype(BF16), wd_ref[sl, :])
    o_ref[...] = x + 0.5 * _rms(acc, g_ref[1:2, :])


def _ffn(x, g, wg, wu, wd):
    n, d = x.shape
    const = lambda i: (0, 0)
    return pl.pallas_call(
        _ffn_kernel,
        out_shape=jax.ShapeDtypeStruct((n, d), F32),
        grid=(n // FFN_ROWS,),
        in_specs=[
            pl.BlockSpec((FFN_ROWS, d), lambda i: (i, 0)),
            pl.BlockSpec((2, d), const),
            pl.BlockSpec((d, D_FF), const, pipeline_mode=pl.Buffered(1)),
            pl.BlockSpec((d, D_FF), const, pipeline_mode=pl.Buffered(1)),
            pl.BlockSpec((D_FF, d), const, pipeline_mode=pl.Buffered(1)),
        ],
        out_specs=pl.BlockSpec((FFN_ROWS, d), lambda i: (i, 0)),
        compiler_params=_params(1),
    )(x, g, wg, wu, wd)


def _gmlp_kernel(x_ref, g_ref, win_ref, lng_ref, lnb_ref, ws_ref, bias_ref, wout_ref,
                 o_ref, m_scr):
    x = x_ref[...]
    rows, d = x.shape
    h = _rms(x, g_ref[0:1, :]).astype(BF16)
    u = _gelu(_dot(h, win_ref[:, :d]))
    v = _gelu(_dot(h, win_ref[:, d:]))
    mu = jnp.mean(v, axis=-1, keepdims=True)
    var = jnp.mean(jnp.square(v - mu), axis=-1, keepdims=True)
    v = ((v - mu) * lax.rsqrt(var + EPS)) * lng_ref[...] + lnb_ref[...]
    ii = lax.broadcasted_iota(jnp.int32, (BLK, BLK), 0)
    jj = lax.broadcasted_iota(jnp.int32, (BLK, BLK), 1)
    causal = jj <= ii
    for grp in range(GMLP_GROUPS):
        lanes = slice(grp * BLK, (grp + 1) * BLK)
        w = jnp.where(causal, ws_ref[grp], 0.0).astype(BF16)
        for c in range(rows // BLK):
            rs = slice(c * BLK, (c + 1) * BLK)
            s = _dot(w, v[rs, lanes].astype(BF16)) + bias_ref[:, lanes]
            m_scr[rs, lanes] = (u[rs, lanes] * s).astype(BF16)
    y = _dot(m_scr[...], wout_ref[...])
    o_ref[...] = x + _rms(y, g_ref[1:2, :])


def _gmlp(x, g, w_in, ln_g, ln_b, w_s, bias_full, w_out):
    n, d = x.shape
    const2 = lambda i: (0, 0)
    return pl.pallas_call(
        _gmlp_kernel,
        out_shape=jax.ShapeDtypeStruct((n, d), F32),
        grid=(n // GMLP_ROWS,),
        in_specs=[
            pl.BlockSpec((GMLP_ROWS, d), lambda i: (i, 0)),
            pl.BlockSpec((2, d), const2),
            pl.BlockSpec((d, 2 * d), const2),
            pl.BlockSpec((1, d), const2),
            pl.BlockSpec((1, d), const2),
            pl.BlockSpec((GMLP_GROUPS, BLK, BLK), lambda i: (0, 0, 0)),
            pl.BlockSpec((BLK, d), const2),
            pl.BlockSpec((d, d), const2),
        ],
        out_specs=pl.BlockSpec((GMLP_ROWS, d), lambda i: (i, 0)),
        scratch_shapes=[pltpu.VMEM((GMLP_ROWS, d), BF16)],
        compiler_params=_params(1),
    )(x, g, w_in, ln_g, ln_b, w_s, bias_full, w_out)


def _rope(z, cos, sin, half):
    lane = lax.broadcasted_iota(jnp.int32, z.shape, 1)
    up = pltpu.roll(z, PAIR - half, axis=1)
    dn = pltpu.roll(z, half, axis=1)
    partner = jnp.where((lane & (HEAD_DIM - 1)) < half, up, dn)
    return z * cos + partner * sin


def _hyb_in_kernel(x_ref, g_ref, w_ref, ca_ref, sa_ref, cr_ref, sr_ref,
                   qa16e_ref, qa16o_ref, ka16_ref, va16_ref,
                   qane_ref, qano_ref, kan_ref, van_ref,
                   qre_ref, qro_ref, kr_ref, vr_ref, gr_ref):
    d = D_MODEL
    x = jnp.concatenate([x_ref[0, :, k * d:(k + 1) * d] for k in range(HYB_RES)], axis=0)
    h = _rms(x, g_ref[...]).astype(BF16)
    ca, sa, cr, sr = ca_ref[...], sa_ref[...], cr_ref[...], sr_ref[...]
    lane = lax.broadcasted_iota(jnp.int32, (x.shape[0], PAIR), 1)
    even = lane < HEAD_DIM

    def proj(group):
        return _dot(h, w_ref[:, group * MIX_W:(group + 1) * MIX_W])

    def put_nat(ref, val, pair):
        for k in range(HYB_RES):
            c0 = k * MIX_W + pair * PAIR
            ref[0, :, c0:c0 + PAIR] = val[k * BLK:(k + 1) * BLK, :].astype(ref.dtype)

    qa_all, ka_all, va_all = proj(0), proj(1), proj(2)
    for p in range(N_PAIRS):
        lanes = slice(p * PAIR, (p + 1) * PAIR)
        qa = _rope(qa_all[:, lanes], ca, sa, ROPE_DIMS // 2) * (HEAD_DIM ** -0.5)
        qa_e = jnp.where(even, qa, 0.0)
        qa_o = jnp.where(even, 0.0, qa)
        qa16e_ref[0, :, lanes] = qa_e.astype(BF16)
        qa16o_ref[0, :, lanes] = qa_o.astype(BF16)
        put_nat(qane_ref, qa_e, p)
        put_nat(qano_ref, qa_o, p)
        ka = _rope(ka_all[:, lanes], ca, sa, ROPE_DIMS // 2)
        ka16_ref[0, :, lanes] = ka.astype(BF16)
        put_nat(kan_ref, ka, p)
        va16_ref[0, :, lanes] = va_all[:, lanes].astype(BF16)
        put_nat(van_ref, va_all[:, lanes], p)
    qr_all, kr_all, vr_all, gr_all = proj(3), proj(4), proj(5), proj(6)
    for p in range(N_PAIRS):
        lanes = slice(p * PAIR, (p + 1) * PAIR)
        qr = _rope(qr_all[:, lanes], cr, sr, HEAD_DIM // 2)
        put_nat(qre_ref, jnp.where(even, qr, 0.0), p)
        put_nat(qro_ref, jnp.where(even, 0.0, qr), p)
        kr = _rope(kr_all[:, lanes], cr, sr, HEAD_DIM // 2) * (HEAD_DIM ** -0.5)
        put_nat(kr_ref, kr, p)
        put_nat(vr_ref, vr_all[:, lanes], p)
        put_nat(gr_ref, gr_all[:, lanes], p)


def _hyb_in(x3, g, w_in, tabs):
    b, nq, _ = x3.shape
    t = nq * DIL
    d = D_MODEL
    rows = HYB_RES * BLK
    sorted_spec = pl.BlockSpec((1, rows, MIX_W), lambda i, j: (i, j, 0))
    nat_spec = pl.BlockSpec((1, nq, HYB_RES * MIX_W), lambda i, j: (i, 0, j))
    tab_spec = pl.BlockSpec((rows, PAIR), lambda i, j: (j, 0))
    sorted_shape = jax.ShapeDtypeStruct((b, t, MIX_W), BF16)
    nat = lambda dt: jax.ShapeDtypeStruct((b, nq, DIL * MIX_W), dt)
    return pl.pallas_call(
        _hyb_in_kernel,
        out_shape=[sorted_shape] * 4 + [nat(BF16)] * 6 + [nat(F32), nat(BF16), nat(F32)],
        grid=(b, DIL // HYB_RES),
        in_specs=[
            pl.BlockSpec((1, nq, HYB_RES * d), lambda i, j: (i, 0, j)),
            pl.BlockSpec((1, d), lambda i, j: (0, 0)),
            pl.BlockSpec(w_in.shape, lambda i, j: (0, 0)),
            tab_spec, tab_spec, tab_spec, tab_spec,
        ],
        out_specs=[sorted_spec] * 4 + [nat_spec] * 9,
        compiler_params=_params(2),
    )(x3, g, w_in, *tabs)


def _pair_select(top_bottom):
    lane = lax.broadcasted_iota(jnp.int32, (BLK, PAIR), 1)
    return jnp.where(lane < HEAD_DIM, top_bottom[:BLK], top_bottom[BLK:])


def _pair_bcast(col):
    lane = lax.broadcasted_iota(jnp.int32, (BLK, PAIR), 1)
    return jnp.where(lane < HEAD_DIM, col[:BLK], col[BLK:])


def _dil_kernel(qe_ref, qo_ref, k_ref, v_ref, o_ref, lse_ref):
    r = pl.program_id(1)
    a = lax.shift_right_logical(r, 2)
    r4 = r & 3
    nk = 4 * BLK
    ii = lax.broadcasted_iota(jnp.int32, (2 * BLK, nk), 0) & (BLK - 1)
    col = lax.broadcasted_iota(jnp.int32, (2 * BLK, nk), 1)
    jj = col & (BLK - 1)
    da = a - lax.shift_right_logical(col, 7)
    delta = 4 * (ii - jj) + da
    in_d4 = jnp.where(jnp.abs(delta - BLK // 2) <= BLK // 2, 1.0, 0.0)
    in_d16 = jnp.where(da == 0, jnp.where(jj <= ii, 1.0, 0.0), 0.0)
    mult = in_d4 + in_d16
    valid = mult > 0.0
    kblk, vblk = [], []
    for ap in range(4):
        start = pl.multiple_of((4 * ap + r4) * BLK, BLK)
        kblk.append(k_ref[0, pl.ds(start, BLK), :])
        vblk.append(v_ref[0, pl.ds(start, BLK), :])
    for p in range(N_PAIRS):
        lanes = slice(p * PAIR, (p + 1) * PAIR)
        lhs = jnp.concatenate([qe_ref[0, :, lanes], qo_ref[0, :, lanes]], axis=0)
        kcat = jnp.concatenate([kb[:, lanes] for kb in kblk], axis=0)
        vcat = jnp.concatenate([vb[:, lanes] for vb in vblk], axis=0)
        s = jnp.where(valid, _dot_nt(lhs, kcat), NEG_INF)
        m = jnp.max(s, axis=-1, keepdims=True)
        pm = jnp.exp(s - m) * mult
        den = jnp.sum(pm, axis=-1, keepdims=True)
        num = _dot(pm.astype(BF16), vcat)
        o_ref[0, :, lanes] = _pair_select(num / den)
        lse_ref[0, :, lanes] = _pair_bcast(m + jnp.log(den))


def _dil(qe, qo, k, v):
    b, t, w = k.shape
    nq = t // DIL
    q_spec = pl.BlockSpec((1, BLK, w), lambda i, j: (i, j, 0))
    kv_spec = pl.BlockSpec((1, t, w), lambda i, j: (i, 0, 0))
    nat_spec = pl.BlockSpec((1, nq, w), lambda i, j: (i, 0, j))
    nat_shape = jax.ShapeDtypeStruct((b, nq, DIL * w), F32)
    return pl.pallas_call(
        _dil_kernel,
        out_shape=[nat_shape, nat_shape],
        grid=(b, DIL),
        in_specs=[q_spec, q_spec, kv_spec, kv_spec],
        out_specs=[nat_spec, nat_spec],
        compiler_params=_params(2),
    )(qe, qo, k, v)


def _hyb_main_kernel(x_ref, g_ref, qe_ref, qo_ref, k_ref, v_ref, o23_ref, l23_ref,
                     qre_ref, qro_ref, kr_ref, vr_ref, gr_ref,
                     decay_ref, zeta_ref, xi_ref, cd_ref, wout_ref,
                     o_ref, state_ref, m_scr):
    step = pl.program_id(1)

    @pl.when(step == 0)
    def _():
        state_ref[...] = jnp.zeros(state_ref.shape, F32)

    lane = lax.broadcasted_iota(jnp.int32, (BLK, PAIR), 1)
    even = lane < HEAD_DIM
    row = lax.broadcasted_iota(jnp.int32, (BLK, PAIR), 0)
    same_head = lax.shift_right_logical(row, 6) == lax.shift_right_logical(lane, 6)
    ii = lax.broadcasted_iota(jnp.int32, (2 * BLK, 2 * BLK), 0) & (BLK - 1)
    jj = lax.broadcasted_iota(jnp.int32, (2 * BLK, 2 * BLK), 1)
    for c in range(MAIN_ROWS // BLK):
        rows = slice(c * BLK, (c + 1) * BLK)
        nb = step * (MAIN_ROWS // BLK) + c
        kstart = pl.multiple_of(jnp.maximum(nb - 1, 0) * BLK, BLK)
        delta = jnp.where(nb > 0, BLK, 0) + ii - jj
        valid = jnp.abs(delta - BLK // 2) <= BLK // 2
        kwin = k_ref[0, pl.ds(kstart, 2 * BLK), :]
        vwin = v_ref[0, pl.ds(kstart, 2 * BLK), :]
        for p in range(N_PAIRS):
            lanes = slice(p * PAIR, (p + 1) * PAIR)
            lhs = jnp.concatenate([qe_ref[0, rows, lanes], qo_ref[0, rows, lanes]], axis=0)
            s = jnp.where(valid, _dot_nt(lhs, kwin[:, lanes]), NEG_INF)
            m1 = jnp.max(s, axis=-1, keepdims=True)
            e = jnp.exp(s - m1)
            den1 = _pair_bcast(jnp.sum(e, axis=-1, keepdims=True))
            num1 = _pair_select(_dot(e.astype(BF16), vwin[:, lanes]))
            m1 = _pair_bcast(m1)
            l23 = l23_ref[0, rows, lanes]
            top = jnp.maximum(l23, m1)
            w23 = jnp.exp(l23 - top)
            w1 = jnp.exp(m1 - top)
            attn = (o23_ref[0, rows, lanes] * w23 + num1 * w1) / (w23 + den1 * w1)
            m_scr[rows, lanes] = attn.astype(BF16)
            qst = jnp.concatenate([qre_ref[0, rows, lanes], qro_ref[0, rows, lanes]], axis=0)
            kr = kr_ref[0, rows, lanes]
            vr = vr_ref[0, rows, lanes]
            scores = _dot_nt(qst, kr.astype(BF16)) * decay_ref[p]
            inner = _pair_select(_dot(scores.astype(BF16), vr))
            state = state_ref[p]
            cross2 = _dot(qst, state.astype(BF16))
            cross = (cross2[:BLK] + cross2[BLK:]) * xi_ref[p]
            kz = (kr * zeta_ref[p]).T.astype(BF16)
            kv = jnp.where(same_head, _dot(kz, vr), 0.0)
            state_ref[p] = state * cd_ref[p] + kv
            out = inner + cross
            sq = out * out
            ms_e = jnp.sum(jnp.where(even, sq, 0.0), axis=-1, keepdims=True)
            ms_o = jnp.sum(jnp.where(even, 0.0, sq), axis=-1, keepdims=True)
            ms = jnp.where(even, ms_e, ms_o) * (1.0 / HEAD_DIM)
            out = out * lax.rsqrt(ms + EPS)
            gate = gr_ref[0, rows, lanes]
            ret = (gate * jax.nn.sigmoid(gate)) * out
            m_scr[rows, MIX_W + p * PAIR:MIX_W + (p + 1) * PAIR] = ret.astype(BF16)
    y = _dot(m_scr[...], wout_ref[...])
    o_ref[0] = x_ref[0] + _rms(y, g_ref[...])


def _hyb_main(x, g, qe, qo, k, v, o23, l23, qre, qro, kr, vr, gr, consts, w_out):
    b, t, d = x.shape
    w = MIX_W
    blk = lambda width: pl.BlockSpec((1, MAIN_ROWS, width), lambda i, j: (i, j, 0))
    full = pl.BlockSpec((1, t, w), lambda i, j: (i, 0, 0))
    const3 = lambda arr: pl.BlockSpec(arr.shape, lambda i, j: (0, 0, 0))
    decay, zeta, xi, cd = consts
    return pl.pallas_call(
        _hyb_main_kernel,
        out_shape=jax.ShapeDtypeStruct((b, t, d), F32),
        grid=(b, t // MAIN_ROWS),
        in_specs=[
            blk(d),
            pl.BlockSpec((1, d), lambda i, j: (0, 0)),
            blk(w), blk(w), full, full, blk(w), blk(w),
            blk(w), blk(w), blk(w), blk(w), blk(w),
            const3(decay), const3(zeta), const3(xi), const3(cd),
            pl.BlockSpec((d, d), lambda i, j: (0, 0)),
        ],
        out_specs=blk(d),
        scratch_shapes=[pltpu.VMEM((N_PAIRS, PAIR, PAIR), F32),
                        pltpu.VMEM((MAIN_ROWS, d), BF16)],
        compiler_params=_params(2),
    )(x, g, qe, qo, k, v, o23, l23, qre, qro, kr, vr, gr, decay, zeta, xi, cd, w_out)


def _rope_table(t, rot_dims, theta):
    half = rot_dims // 2
    inv = theta ** (-(jnp.arange(half, dtype=F32) * 2.0 / rot_dims))
    pos = (np.arange(t) % (t // DIL)) * DIL + np.arange(t) // (t // DIL)
    ang = jnp.asarray(pos, dtype=F32)[:, None] * inv[None, :]
    cos, sin = jnp.cos(ang), jnp.sin(ang)
    rest = HEAD_DIM - rot_dims
    cos64 = jnp.concatenate([cos, cos, jnp.ones((t, rest), F32)], axis=-1)
    sin64 = jnp.concatenate([-sin, sin, jnp.zeros((t, rest), F32)], axis=-1)
    return jnp.tile(cos64, (1, 2)), jnp.tile(sin64, (1, 2))


def _retention_consts(n_heads):
    c = BLK
    log_g = jnp.log(1.0 - jnp.exp2(-5.0 - jnp.arange(n_heads, dtype=F32)))
    idx = jnp.arange(c, dtype=F32)
    diff = idx[:, None] - idx[None, :]
    decay = jnp.where(diff >= 0, jnp.exp(log_g[:, None, None] * jnp.maximum(diff, 0.0)), 0.0)
    zeta = jnp.exp(log_g[:, None] * (c - 1.0 - idx)[None, :])
    xi = jnp.exp(log_g[:, None] * (idx + 1.0)[None, :])
    chunk_decay = jnp.exp(log_g * c)
    per_lane = lambda hc: jnp.repeat(
        hc.reshape(N_PAIRS, 2, -1).transpose(0, 2, 1), HEAD_DIM, axis=-1)
    decay_st = decay.reshape(N_PAIRS, 2 * c, c)
    return decay_st, per_lane(zeta), per_lane(xi), per_lane(chunk_decay[:, None])


def _hybrid_layer(x, g_in, g_out, w_in, w_out, tabs, consts):
    b, t, d = x.shape
    nq = t // DIL
    outs = _hyb_in(x.reshape(b, nq, DIL * d), g_in, w_in, tabs)
    qa16e, qa16o, ka16, va16 = outs[:4]
    qane, qano, kan, van, qre, qro, kr, vr, gr = [o.reshape(b, t, MIX_W) for o in outs[4:]]
    o23, l23 = _dil(qa16e, qa16o, ka16, va16)
    return _hyb_main(x, g_out, qane, qano, kan, van,
                     o23.reshape(b, t, MIX_W), l23.reshape(b, t, MIX_W),
                     qre, qro, kr, vr, gr, consts, w_out)


def kernel(x, norm_g, ffn_w_gate, ffn_w_up, ffn_w_down, hyb_w_in, hyb_w_out,
           gmlp_w_in, gmlp_ln_g, gmlp_ln_b, gmlp_w_s, gmlp_b_s, gmlp_w_out):
    b, t, d = x.shape
    depth = norm_g.shape[0]
    wg, wu, wd = (w.astype(BF16) for w in (ffn_w_gate, ffn_w_up, ffn_w_down))
    hyb_in, hyb_out = hyb_w_in.astype(BF16), hyb_w_out.astype(BF16)
    g_in, g_out = gmlp_w_in.astype(BF16), gmlp_w_out.astype(BF16)
    tabs = _rope_table(t, ROPE_DIMS, ROPE_THETA) + _rope_table(t, HEAD_DIM, RET_ROPE_THETA)
    consts = _retention_consts(2 * N_PAIRS)
    for layer in range(depth):
        g = norm_g[layer]
        j = layer // 2
        x = _ffn(x.reshape(b * t, d), g[0:2], wg[layer, 0], wu[layer, 0], wd[layer, 0])
        if layer % 2 == 0:
            x = _hybrid_layer(x.reshape(b, t, d), g[2:3], g[3:4], hyb_in[j], hyb_out[j],
                              tabs, consts)
        else:
            bias_full = jnp.repeat(gmlp_b_s[j].T, BLK, axis=1)
            x = _gmlp(x.reshape(b * t, d), g[2:4], g_in[j], gmlp_ln_g[j][None, :],
                      gmlp_ln_b[j][None, :], gmlp_w_s[j], bias_full, g_out[j])
        x = _ffn(x.reshape(b * t, d), g[4:6], wg[layer, 1], wu[layer, 1], wd[layer, 1])
    return x.reshape(b, t, d)
```

```python
import functools

import jax
import jax.numpy as jnp
import numpy as np
from jax import lax
from jax.experimental import pallas as pl
from jax.experimental.pallas import tpu as pltpu

F32 = jnp.float32
BF16 = jnp.bfloat16

D_MODEL = 1024
HEAD_DIM = 64
PAIR = 2 * HEAD_DIM
N_PAIRS = 4
MIX_W = N_PAIRS * PAIR
BLK = 128
DIL = 16
D_FF = 2816
FF_CHUNK = 256
GMLP_GROUPS = 8
ROPE_THETA = 500000.0
ROPE_DIMS = HEAD_DIM // 4
RET_ROPE_THETA = 10000.0
EPS = 1e-6
NEG_INF = -1e30

FFN_ROWS = 512
GMLP_ROWS = 256
IN_ROWS = 512
MAIN_ROWS = 256
VMEM_LIMIT = 56 * 1024 * 1024


def _dot(a, b):
    return jnp.dot(a, b, preferred_element_type=F32)


def _dot_nt(a, b):
    return lax.dot_general(a, b, (((1,), (1,)), ((), ())), preferred_element_type=F32)


def _rms(x, g):
    y = x * lax.rsqrt(jnp.mean(x * x, axis=-1, keepdims=True) + EPS)
    return y * g


def _gelu(x):
    return 0.5 * x * (1.0 + lax.erf(x * np.float32(np.sqrt(0.5))))


def _params(n_axes):
    return pltpu.CompilerParams(
        dimension_semantics=("arbitrary",) * n_axes, vmem_limit_bytes=VMEM_LIMIT)


def _ffn_kernel(x_ref, g_ref, wg_ref, wu_ref, wd_ref, o_ref, *, g_row):
    x = x_ref[...]
    h = _rms(x, g_ref[g_row:g_row + 1, :]).astype(BF16)
    acc = jnp.zeros(x.shape, F32)
    for c in range(D_FF // FF_CHUNK):
        sl = slice(c * FF_CHUNK, (c + 1) * FF_CHUNK)
        gate = _dot(h, wg_ref[:, sl])
        up = _dot(h, wu_ref[:, sl])
        act = (gate * jax.nn.sigmoid(gate)) * up
        acc = acc + _dot(act.astype(BF16), wd_ref[sl, :])
    o_ref[...] = x + 0.5 * _rms(acc, g_ref[g_row + 1:g_row + 2, :])


def _ffn(x, norm_g, wg, wu, wd, layer, which):
    n, d = x.shape
    n_g = norm_g.shape[1]
    pick = lambda i: (layer, which, 0, 0)
    resident = dict(pipeline_mode=pl.Buffered(1))
    return pl.pallas_call(
        functools.partial(_ffn_kernel, g_row=4 * which),
        out_shape=jax.ShapeDtypeStruct((n, d), F32),
        grid=(n // FFN_ROWS,),
        in_specs=[
            pl.BlockSpec((FFN_ROWS, d), lambda i: (i, 0)),
            pl.BlockSpec((None, n_g, d), lambda i: (layer, 0, 0)),
            pl.BlockSpec((None, None, d, D_FF), pick, **resident),
            pl.BlockSpec((None, None, d, D_FF), pick, **resident),
            pl.BlockSpec((None, None, D_FF, d), pick, **resident),
        ],
        out_specs=pl.BlockSpec((FFN_ROWS, d), lambda i: (i, 0)),
        compiler_params=_params(1),
    )(x, norm_g, wg, wu, wd)


def _gmlp_kernel(x_ref, g_ref, win_ref, lng_ref, lnb_ref, ws_ref, bias_ref, wout_ref,
                 o_ref, m_scr):
    x = x_ref[...]
    rows, d = x.shape
    h = _rms(x, g_ref[2:3, :]).astype(BF16)
    u = _gelu(_dot(h, win_ref[:, :d]))
    v = _gelu(_dot(h, win_ref[:, d:]))
    mu = jnp.mean(v, axis=-1, keepdims=True)
    var = jnp.mean(jnp.square(v - mu), axis=-1, keepdims=True)
    v = ((v - mu) * lax.rsqrt(var + EPS)) * lng_ref[...] + lnb_ref[...]
    ii = lax.broadcasted_iota(jnp.int32, (BLK, BLK), 0)
    jj = lax.broadcasted_iota(jnp.int32, (BLK, BLK), 1)
    causal = jj <= ii
    for grp in range(GMLP_GROUPS):
        lanes = slice(grp * BLK, (grp + 1) * BLK)
        w = jnp.where(causal, ws_ref[grp], 0.0).astype(BF16)
        for c in range(rows // BLK):
            rs = slice(c * BLK, (c + 1) * BLK)
            s = _dot(w, v[rs, lanes].astype(BF16)) + bias_ref[:, lanes]
            m_scr[rs, lanes] = (u[rs, lanes] * s).astype(BF16)
    y = _dot(m_scr[...], wout_ref[...])
    o_ref[...] = x + _rms(y, g_ref[3:4, :])


def _gmlp(x, norm_g, w_in, ln_g, ln_b, w_s, bias_full, w_out, layer, j):
    n, d = x.shape
    n_g = norm_g.shape[1]
    sel = lambda i: (j, 0, 0)
    return pl.pallas_call(
        _gmlp_kernel,
        out_shape=jax.ShapeDtypeStruct((n, d), F32),
        grid=(n // GMLP_ROWS,),
        in_specs=[
            pl.BlockSpec((GMLP_ROWS, d), lambda i: (i, 0)),
            pl.BlockSpec((None, n_g, d), lambda i: (layer, 0, 0)),
            pl.BlockSpec((None, d, 2 * d), sel),
            pl.BlockSpec((None, 1, d), sel),
            pl.BlockSpec((None, 1, d), sel),
            pl.BlockSpec((None, GMLP_GROUPS, BLK, BLK), lambda i: (j, 0, 0, 0)),
            pl.BlockSpec((None, BLK, d), sel),
            pl.BlockSpec((None, d, d), sel),
        ],
        out_specs=pl.BlockSpec((GMLP_ROWS, d), lambda i: (i, 0)),
        scratch_shapes=[pltpu.VMEM((GMLP_ROWS, d), BF16)],
        compiler_params=_params(1),
    )(x, norm_g, w_in, ln_g, ln_b, w_s, bias_full, w_out)


def _rope(z, cos, sin, half):
    lane = lax.broadcasted_iota(jnp.int32, z.shape, 1)
    up = pltpu.roll(z, PAIR - half, axis=1)
    dn = pltpu.roll(z, half, axis=1)
    partner = jnp.where((lane & (HEAD_DIM - 1)) < half, up, dn)
    return z * cos + partner * sin


def _hyb_in_kernel(x_ref, g_ref, w_ref, ca_ref, sa_ref, cr_ref, sr_ref,
                   qae_ref, qao_ref, ka_ref, va_ref,
                   q16e_ref, q16o_ref, k16_ref, v16_ref,
                   qre_ref, qro_ref, kr_ref, vr_ref, gr_ref, sort_scr):
    h = _rms(x_ref[...], g_ref[2:3, :]).astype(BF16)
    ca, sa, cr, sr = ca_ref[...], sa_ref[...], cr_ref[...], sr_ref[...]
    even = lax.broadcasted_iota(jnp.int32, (IN_ROWS, PAIR), 1) < HEAD_DIM
    even_s = lax.broadcasted_iota(jnp.int32, (IN_ROWS // DIL, PAIR), 1) < HEAD_DIM

    def proj(group):
        return _dot(h, w_ref[:, group * MIX_W:(group + 1) * MIX_W])

    def sorted_rows(val):
        sort_scr[...] = val
        return [sort_scr[pl.ds(r, IN_ROWS // DIL, stride=DIL), :] for r in range(DIL)]

    qa_all, ka_all, va_all = proj(0), proj(1), proj(2)
    for p in range(N_PAIRS):
        lanes = slice(p * PAIR, (p + 1) * PAIR)
        qa = _rope(qa_all[:, lanes], ca, sa, ROPE_DIMS // 2) * (HEAD_DIM ** -0.5)
        qae_ref[0, p] = jnp.where(even, qa, 0.0).astype(BF16)
        qao_ref[0, p] = jnp.where(even, 0.0, qa).astype(BF16)
        for r, rows in enumerate(sorted_rows(qa)):
            q16e_ref[0, p, r] = jnp.where(even_s, rows, 0.0).astype(BF16)
            q16o_ref[0, p, r] = jnp.where(even_s, 0.0, rows).astype(BF16)
        ka = _rope(ka_all[:, lanes], ca, sa, ROPE_DIMS // 2)
        ka_ref[0, p] = ka.astype(BF16)
        for r, rows in enumerate(sorted_rows(ka)):
            k16_ref[0, p, r] = rows.astype(BF16)
        va = va_all[:, lanes]
        va_ref[0, p] = va.astype(BF16)
        for r, rows in enumerate(sorted_rows(va)):
            v16_ref[0, p, r] = rows.astype(BF16)
    qr_all, kr_all, vr_all, gr_all = proj(3), proj(4), proj(5), proj(6)
    for p in range(N_PAIRS):
        lanes = slice(p * PAIR, (p + 1) * PAIR)
        qr = _rope(qr_all[:, lanes], cr, sr, HEAD_DIM // 2)
        qre_ref[0, p] = jnp.where(even, qr, 0.0).astype(BF16)
        qro_ref[0, p] = jnp.where(even, 0.0, qr).astype(BF16)
        kr_ref[0, p] = _rope(kr_all[:, lanes], cr, sr, HEAD_DIM // 2) * (HEAD_DIM ** -0.5)
        vr_ref[0, p] = vr_all[:, lanes].astype(BF16)
        gr_ref[0, p] = gr_all[:, lanes]


def _hyb_in(x, norm_g, w_in, tabs, layer, j, b, t):
    n, d = x.shape
    n_g = norm_g.shape[1]
    per_b = t // IN_ROWS
    nat_spec = pl.BlockSpec((1, N_PAIRS, IN_ROWS, PAIR), lambda i: (i // per_b, 0, i % per_b, 0))
    sorted_spec = pl.BlockSpec((1, N_PAIRS, DIL, IN_ROWS // DIL, PAIR),
                               lambda i: (i // per_b, 0, 0, i % per_b, 0))
    tab_spec = pl.BlockSpec((IN_ROWS, PAIR), lambda i: (i % per_b, 0))
    nat = lambda dt: jax.ShapeDtypeStruct((b, N_PAIRS, t, PAIR), dt)
    srt = jax.ShapeDtypeStruct((b, N_PAIRS, DIL, t // DIL, PAIR), BF16)
    return pl.pallas_call(
        _hyb_in_kernel,
        out_shape=[nat(BF16)] * 4 + [srt] * 4 + [nat(BF16), nat(BF16), nat(F32), nat(BF16), nat(F32)],
        grid=(n // IN_ROWS,),
        in_specs=[
            pl.BlockSpec((IN_ROWS, d), lambda i: (i, 0)),
            pl.BlockSpec((None, n_g, d), lambda i: (layer, 0, 0)),
            pl.BlockSpec((None,) + w_in.shape[1:], lambda i: (j, 0, 0)),
            tab_spec, tab_spec, tab_spec, tab_spec,
        ],
        out_specs=[nat_spec] * 4 + [sorted_spec] * 4 + [nat_spec] * 5,
        scratch_shapes=[pltpu.VMEM((IN_ROWS, PAIR), F32)],
        compiler_params=_params(1),
    )(x, norm_g, w_in, *tabs)


def _pair_select(top_bottom):
    lane = lax.broadcasted_iota(jnp.int32, (BLK, PAIR), 1)
    return jnp.where(lane < HEAD_DIM, top_bottom[:BLK], top_bottom[BLK:])


def _pair_bcast(col):
    lane = lax.broadcasted_iota(jnp.int32, (BLK, PAIR), 1)
    return jnp.where(lane < HEAD_DIM, col[:BLK], col[BLK:])


def _dil_kernel(qe_ref, qo_ref, k_ref, v_ref, o_ref, lse_ref):
    r = pl.program_id(1)
    a = lax.shift_right_logical(r, 2)
    r4 = r & 3
    nk = 4 * BLK
    ii = lax.broadcasted_iota(jnp.int32, (2 * BLK, nk), 0) & (BLK - 1)
    col = lax.broadcasted_iota(jnp.int32, (2 * BLK, nk), 1)
    jj = col & (BLK - 1)
    da = a - lax.shift_right_logical(col, 7)
    delta = 4 * (ii - jj) + da
    in_d4 = jnp.where(jnp.abs(delta - BLK // 2) <= BLK // 2, 1.0, 0.0)
    in_d16 = jnp.where(da == 0, jnp.where(jj <= ii, 1.0, 0.0), 0.0)
    mult = in_d4 + in_d16
    valid = mult > 0.0
    for p in range(N_PAIRS):
        lhs = jnp.concatenate([qe_ref[0, p, 0], qo_ref[0, p, 0]], axis=0)
        kcat = jnp.concatenate([k_ref[0, p, 4 * ap + r4] for ap in range(4)], axis=0)
        vcat = jnp.concatenate([v_ref[0, p, 4 * ap + r4] for ap in range(4)], axis=0)
        s = jnp.where(valid, _dot_nt(lhs, kcat), NEG_INF)
        m = jnp.max(s, axis=-1, keepdims=True)
        pm = jnp.exp(s - m) * mult
        den = jnp.sum(pm, axis=-1, keepdims=True)
        num = _dot(pm.astype(BF16), vcat)
        o_ref[0, p, pl.ds(r, BLK, stride=DIL), :] = _pair_select(num / den)
        lse_ref[0, p, pl.ds(r, BLK, stride=DIL), :] = _pair_bcast(m + jnp.log(den))


def _dil(qe, qo, k, v):
    b, _, _, nq, _ = k.shape
    t = nq * DIL
    q_spec = pl.BlockSpec((1, N_PAIRS, 1, nq, PAIR), lambda i, j: (i, 0, j, 0, 0))
    kv_spec = pl.BlockSpec((1, N_PAIRS, DIL, nq, PAIR), lambda i, j: (i, 0, 0, 0, 0))
    nat_spec = pl.BlockSpec((1, N_PAIRS, t, PAIR), lambda i, j: (i, 0, 0, 0))
    nat_shape = jax.ShapeDtypeStruct((b, N_PAIRS, t, PAIR), F32)
    return pl.pallas_call(
        _dil_kernel,
        out_shape=[nat_shape, nat_shape],
        grid=(b, DIL),
        in_specs=[q_spec, q_spec, kv_spec, kv_spec],
        out_specs=[nat_spec, nat_spec],
        compiler_params=_params(2),
    )(qe, qo, k, v)


def _hyb_main_kernel(x_ref, g_ref, qe_ref, qo_ref, k_ref, v_ref, o23_ref, l23_ref,
                     qre_ref, qro_ref, kr_ref, vr_ref, gr_ref,
                     decay_ref, zeta_ref, xi_ref, cd_ref, wout_ref,
                     o_ref, state_ref, m_scr):
    step = pl.program_id(1)

    @pl.when(step == 0)
    def _():
        state_ref[...] = jnp.zeros(state_ref.shape, F32)

    lane = lax.broadcasted_iota(jnp.int32, (BLK, PAIR), 1)
    even = lane < HEAD_DIM
    row = lax.broadcasted_iota(jnp.int32, (BLK, PAIR), 0)
    same_head = lax.shift_right_logical(row, 6) == lax.shift_right_logical(lane, 6)
    ii = lax.broadcasted_iota(jnp.int32, (2 * BLK, 2 * BLK), 0) & (BLK - 1)
    jj = lax.broadcasted_iota(jnp.int32, (2 * BLK, 2 * BLK), 1)
    for c in range(MAIN_ROWS // BLK):
        rows = slice(c * BLK, (c + 1) * BLK)
        nb = step * (MAIN_ROWS // BLK) + c
        kstart = pl.multiple_of(jnp.maximum(nb - 1, 0) * BLK, BLK)
        delta = jnp.where(nb > 0, BLK, 0) + ii - jj
        valid = jnp.abs(delta - BLK // 2) <= BLK // 2
        for p in range(N_PAIRS):
            lanes = slice(p * PAIR, (p + 1) * PAIR)
            lhs = jnp.concatenate([qe_ref[0, p, rows, :], qo_ref[0, p, rows, :]], axis=0)
            kwin = k_ref[0, p, pl.ds(kstart, 2 * BLK), :]
            vwin = v_ref[0, p, pl.ds(kstart, 2 * BLK), :]
            s = jnp.where(valid, _dot_nt(lhs, kwin), NEG_INF)
            m1 = jnp.max(s, axis=-1, keepdims=True)
            e = jnp.exp(s - m1)
            den1 = _pair_bcast(jnp.sum(e, axis=-1, keepdims=True))
            num1 = _pair_select(_dot(e.astype(BF16), vwin))
            m1 = _pair_bcast(m1)
            l23 = l23_ref[0, p, rows, :]
            top = jnp.maximum(l23, m1)
            w23 = jnp.exp(l23 - top)
            w1 = jnp.exp(m1 - top)
            attn = (o23_ref[0, p, rows, :] * w23 + num1 * w1) / (w23 + den1 * w1)
            m_scr[rows, lanes] = attn.astype(BF16)
            qst = jnp.concatenate([qre_ref[0, p, rows, :], qro_ref[0, p, rows, :]], axis=0)
            kr = kr_ref[0, p, rows, :]
            vr = vr_ref[0, p, rows, :]
            scores = _dot_nt(qst, kr.astype(BF16)) * decay_ref[p]
            inner = _pair_select(_dot(scores.astype(BF16), vr))
            state = state_ref[p]
            cross2 = _dot(qst, state.astype(BF16))
            cross = (cross2[:BLK] + cross2[BLK:]) * xi_ref[p]
            kz = (kr * zeta_ref[p]).T.astype(BF16)
            kv = jnp.where(same_head, _dot(kz, vr), 0.0)
            state_ref[p] = state * cd_ref[p] + kv
            out = inner + cross
            sq = out * out
            ms_e = jnp.sum(jnp.where(even, sq, 0.0), axis=-1, keepdims=True)
            ms_o = jnp.sum(jnp.where(even, 0.0, sq), axis=-1, keepdims=True)
            ms = jnp.where(even, ms_e, ms_o) * (1.0 / HEAD_DIM)
            out = out * lax.rsqrt(ms + EPS)
            gate = gr_ref[0, p, rows, :]
            ret = (gate * jax.nn.sigmoid(gate)) * out
            m_scr[rows, MIX_W + p * PAIR:MIX_W + (p + 1) * PAIR] = ret.astype(BF16)
    y = _dot(m_scr[...], wout_ref[...])
    o_ref[0] = x_ref[0] + _rms(y, g_ref[3:4, :])


def _hyb_main(x, norm_g, qe, qo, k, v, o23, l23, qre, qro, kr, vr, gr, consts, w_out, layer, j):
    b, t, d = x.shape
    n_g = norm_g.shape[1]
    blk = pl.BlockSpec((1, N_PAIRS, MAIN_ROWS, PAIR), lambda i, s: (i, 0, s, 0))
    full = pl.BlockSpec((1, N_PAIRS, t, PAIR), lambda i, s: (i, 0, 0, 0))
    xblk = pl.BlockSpec((1, MAIN_ROWS, d), lambda i, s: (i, s, 0))
    const3 = lambda arr: pl.BlockSpec(arr.shape, lambda i, s: (0, 0, 0))
    decay, zeta, xi, cd = consts
    return pl.pallas_call(
        _hyb_main_kernel,
        out_shape=jax.ShapeDtypeStruct((b, t, d), F32),
        grid=(b, t // MAIN_ROWS),
        in_specs=[
            xblk,
            pl.BlockSpec((None, n_g, d), lambda i, s: (layer, 0, 0)),
            blk, blk, full, full, blk, blk,
            blk, blk, blk, blk, blk,
            const3(decay), const3(zeta), const3(xi), const3(cd),
            pl.BlockSpec((None, d, d), lambda i, s: (j, 0, 0)),
        ],
        out_specs=xblk,
        scratch_shapes=[pltpu.VMEM((N_PAIRS, PAIR, PAIR), F32),
                        pltpu.VMEM((MAIN_ROWS, d), BF16)],
        compiler_params=_params(2),
    )(x, norm_g, qe, qo, k, v, o23, l23, qre, qro, kr, vr, gr, decay, zeta, xi, cd, w_out)


def _rope_table(t, rot_dims, theta):
    half = rot_dims // 2
    inv = theta ** (-(jnp.arange(half, dtype=F32) * 2.0 / rot_dims))
    ang = jnp.arange(t, dtype=F32)[:, None] * inv[None, :]
    cos, sin = jnp.cos(ang), jnp.sin(ang)
    rest = HEAD_DIM - rot_dims
    cos64 = jnp.concatenate([cos, cos, jnp.ones((t, rest), F32)], axis=-1)
    sin64 = jnp.concatenate([-sin, sin, jnp.zeros((t, rest), F32)], axis=-1)
    return jnp.tile(cos64, (1, 2)), jnp.tile(sin64, (1, 2))


def _retention_consts(n_heads):
    c = BLK
    log_g = jnp.log(1.0 - jnp.exp2(-5.0 - jnp.arange(n_heads, dtype=F32)))
    idx = jnp.arange(c, dtype=F32)
    diff = idx[:, None] - idx[None, :]
    decay = jnp.where(diff >= 0, jnp.exp(log_g[:, None, None] * jnp.maximum(diff, 0.0)), 0.0)
    zeta = jnp.exp(log_g[:, None] * (c - 1.0 - idx)[None, :])
    xi = jnp.exp(log_g[:, None] * (idx + 1.0)[None, :])
    chunk_decay = jnp.exp(log_g * c)
    per_lane = lambda hc: jnp.repeat(
        hc.reshape(N_PAIRS, 2, -1).transpose(0, 2, 1), HEAD_DIM, axis=-1)
    decay_st = decay.reshape(N_PAIRS, 2 * c, c)
    return decay_st, per_lane(zeta), per_lane(xi), per_lane(chunk_decay[:, None])


def kernel(x, norm_g, ffn_w_gate, ffn_w_up, ffn_w_down, hyb_w_in, hyb_w_out,
           gmlp_w_in, gmlp_ln_g, gmlp_ln_b, gmlp_w_s, gmlp_b_s, gmlp_w_out):
    b, t, d = x.shape
    depth = norm_g.shape[0]
    wg, wu, wd = (w.astype(BF16) for w in (ffn_w_gate, ffn_w_up, ffn_w_down))
    hyb_in, hyb_out = hyb_w_in.astype(BF16), hyb_w_out.astype(BF16)
    g_in, g_out = gmlp_w_in.astype(BF16), gmlp_w_out.astype(BF16)
    tabs = _rope_table(t, ROPE_DIMS, ROPE_THETA) + _rope_table(t, HEAD_DIM, RET_ROPE_THETA)
    consts = _retention_consts(2 * N_PAIRS)
    bias_full = jnp.repeat(jnp.swapaxes(gmlp_b_s, 1, 2), BLK, axis=2)
    ln_g, ln_b = gmlp_ln_g[:, None, :], gmlp_ln_b[:, None, :]
    x = x.reshape(b * t, d)
    for layer in range(depth):
        j = layer // 2
        x = _ffn(x, norm_g, wg, wu, wd, layer, 0)
        if layer % 2 == 0:
            (qae, qao, ka, va, q16e, q16o, k16, v16, qre, qro, kr, vr, gr) = _hyb_in(
                x, norm_g, hyb_in, tabs, layer, j, b, t)
            o23, l23 = _dil(q16e, q16o, k16, v16)
            x = _hyb_main(x.reshape(b, t, d), norm_g, qae, qao, ka, va, o23, l23,
                          qre, qro, kr, vr, gr, consts, hyb_out, layer, j).reshape(b * t, d)
        else:
            x = _gmlp(x, norm_g, g_in, ln_g, ln_b, gmlp_w_s, bias_full, g_out, layer, j)
        x = _ffn(x, norm_g, wg, wu, wd, layer, 1)
    return x.reshape(b, t, d)
```

```python
import functools

import jax
import jax.numpy as jnp
import numpy as np
from jax import lax
from jax.experimental import pallas as pl
from jax.experimental.pallas import tpu as pltpu

F32 = jnp.float32
BF16 = jnp.bfloat16

D_MODEL = 1024
HEAD_DIM = 64
PAIR = 2 * HEAD_DIM
N_PAIRS = 4
MIX_W = N_PAIRS * PAIR
BLK = 128
DIL = 16
D_FF = 2816
FF_CHUNK = 256
GMLP_GROUPS = 8
ROPE_THETA = 500000.0
ROPE_DIMS = HEAD_DIM // 4
RET_ROPE_THETA = 10000.0
EPS = 1e-6
NEG_INF = -1e30
LOG2_E = np.float32(np.log2(np.e))
QK_SCALE = np.float32(HEAD_DIM ** -0.5 * np.log2(np.e))

FFN_ROWS = 1024
FFN_SUB = 512
GMLP_ROWS = 256
IN_ROWS = 512
MAIN_ROWS = 256
DIL_GROUP = 2
VMEM_LIMIT = 56 * 1024 * 1024


def _dot(a, b):
    return jnp.dot(a, b, preferred_element_type=F32)


def _dot_nt(a, b):
    return lax.dot_general(a, b, (((1,), (1,)), ((), ())), preferred_element_type=F32)


def _rms(x, g):
    y = x * lax.rsqrt(jnp.mean(x * x, axis=-1, keepdims=True) + EPS)
    return y * g


def _gelu(x):
    return 0.5 * x * (1.0 + lax.erf(x * np.float32(np.sqrt(0.5))))


def _params(n_axes):
    return pltpu.CompilerParams(
        dimension_semantics=("arbitrary",) * n_axes, vmem_limit_bytes=VMEM_LIMIT)


def _ffn_kernel(x_ref, g_ref, wg_ref, wu_ref, wd_ref, o_ref, *, g_row):
    for sub in range(FFN_ROWS // FFN_SUB):
        rows = slice(sub * FFN_SUB, (sub + 1) * FFN_SUB)
        x = x_ref[rows, :]
        h = _rms(x, g_ref[g_row:g_row + 1, :]).astype(BF16)
        acc = jnp.zeros(x.shape, F32)
        for c in range(D_FF // FF_CHUNK):
            sl = slice(c * FF_CHUNK, (c + 1) * FF_CHUNK)
            gate = _dot(h, wg_ref[:, sl])
            up = _dot(h, wu_ref[:, sl])
            act = (gate * jax.nn.sigmoid(gate)) * up
            acc = acc + _dot(act.astype(BF16), wd_ref[sl, :])
        o_ref[rows, :] = x + 0.5 * _rms(acc, g_ref[g_row + 1:g_row + 2, :])


def _ffn(x, norm_g, wg, wu, wd, layer, which):
    n, d = x.shape
    n_g = norm_g.shape[1]
    pick = lambda i: (layer, which, 0, 0)
    resident = dict(pipeline_mode=pl.Buffered(1))
    return pl.pallas_call(
        functools.partial(_ffn_kernel, g_row=4 * which),
        out_shape=jax.ShapeDtypeStruct((n, d), F32),
        grid=(n // FFN_ROWS,),
        in_specs=[
            pl.BlockSpec((FFN_ROWS, d), lambda i: (i, 0)),
            pl.BlockSpec((None, n_g, d), lambda i: (layer, 0, 0)),
            pl.BlockSpec((None, None, d, D_FF), pick, **resident),
            pl.BlockSpec((None, None, d, D_FF), pick, **resident),
            pl.BlockSpec((None, None, D_FF, d), pick, **resident),
        ],
        out_specs=pl.BlockSpec((FFN_ROWS, d), lambda i: (i, 0)),
        compiler_params=_params(1),
    )(x, norm_g, wg, wu, wd)


def _gmlp_kernel(x_ref, g_ref, win_ref, lng_ref, lnb_ref, ws_ref, bias_ref, wout_ref,
                 o_ref, m_scr):
    x = x_ref[...]
    rows, d = x.shape
    h = _rms(x, g_ref[2:3, :]).astype(BF16)
    u = _gelu(_dot(h, win_ref[:, :d]))
    v = _gelu(_dot(h, win_ref[:, d:]))
    mu = jnp.mean(v, axis=-1, keepdims=True)
    var = jnp.mean(jnp.square(v - mu), axis=-1, keepdims=True)
    v = ((v - mu) * lax.rsqrt(var + EPS)) * lng_ref[...] + lnb_ref[...]
    ii = lax.broadcasted_iota(jnp.int32, (BLK, BLK), 0)
    jj = lax.broadcasted_iota(jnp.int32, (BLK, BLK), 1)
    causal = jj <= ii
    for grp in range(GMLP_GROUPS):
        lanes = slice(grp * BLK, (grp + 1) * BLK)
        w = jnp.where(causal, ws_ref[grp], 0.0).astype(BF16)
        for c in range(rows // BLK):
            rs = slice(c * BLK, (c + 1) * BLK)
            s = _dot(w, v[rs, lanes].astype(BF16)) + bias_ref[:, lanes]
            m_scr[rs, lanes] = (u[rs, lanes] * s).astype(BF16)
    y = _dot(m_scr[...], wout_ref[...])
    o_ref[...] = x + _rms(y, g_ref[3:4, :])


def _gmlp(x, norm_g, w_in, ln_g, ln_b, w_s, bias_full, w_out, layer, j):
    n, d = x.shape
    n_g = norm_g.shape[1]
    sel = lambda i: (j, 0, 0)
    return pl.pallas_call(
        _gmlp_kernel,
        out_shape=jax.ShapeDtypeStruct((n, d), F32),
        grid=(n // GMLP_ROWS,),
        in_specs=[
            pl.BlockSpec((GMLP_ROWS, d), lambda i: (i, 0)),
            pl.BlockSpec((None, n_g, d), lambda i: (layer, 0, 0)),
            pl.BlockSpec((None, d, 2 * d), sel),
            pl.BlockSpec((None, 1, d), sel),
            pl.BlockSpec((None, 1, d), sel),
            pl.BlockSpec((None, GMLP_GROUPS, BLK, BLK), lambda i: (j, 0, 0, 0)),
            pl.BlockSpec((None, BLK, d), sel),
            pl.BlockSpec((None, d, d), sel),
        ],
        out_specs=pl.BlockSpec((GMLP_ROWS, d), lambda i: (i, 0)),
        scratch_shapes=[pltpu.VMEM((GMLP_ROWS, d), BF16)],
        compiler_params=_params(1),
    )(x, norm_g, w_in, ln_g, ln_b, w_s, bias_full, w_out)


def _rope(z, cos, sin, half):
    lane = lax.broadcasted_iota(jnp.int32, z.shape, 1)
    up = pltpu.roll(z, PAIR - half, axis=1)
    dn = pltpu.roll(z, half, axis=1)
    partner = jnp.where((lane & (HEAD_DIM - 1)) < half, up, dn)
    return z * cos + partner * sin


def _hyb_in_kernel(x_ref, g_ref, w_ref, ca_ref, sa_ref, cr_ref, sr_ref,
                   qae_ref, qao_ref, ka_ref, va_ref,
                   q16e_ref, q16o_ref, k16_ref, v16_ref,
                   qre_ref, qro_ref, kr_ref, vr_ref, gr_ref, sort_scr, sort2_scr):
    h = _rms(x_ref[...], g_ref[2:3, :]).astype(BF16)
    ca, sa, cr, sr = ca_ref[...], sa_ref[...], cr_ref[...], sr_ref[...]
    even = lax.broadcasted_iota(jnp.int32, (IN_ROWS, PAIR), 1) < HEAD_DIM
    even_s = lax.broadcasted_iota(jnp.int32, (IN_ROWS // DIL, PAIR), 1) < HEAD_DIM

    def proj(group):
        return _dot(h, w_ref[:, group * MIX_W:(group + 1) * MIX_W])

    def sorted_rows(val):
        sort_scr[...] = val
        quarter = IN_ROWS // 4
        for r4 in range(4):
            sort2_scr[r4 * quarter:(r4 + 1) * quarter, :] = sort_scr[pl.ds(r4, quarter, stride=4), :]
        return [sort2_scr[pl.ds((r % 4) * quarter + r // 4, IN_ROWS // DIL, stride=4), :]
                for r in range(DIL)]

    qa_all, ka_all, va_all = proj(0), proj(1), proj(2)
    for p in range(N_PAIRS):
        lanes = slice(p * PAIR, (p + 1) * PAIR)
        qa = _rope(qa_all[:, lanes], ca, sa, ROPE_DIMS // 2) * QK_SCALE
        qae_ref[0, p] = jnp.where(even, qa, 0.0).astype(BF16)
        qao_ref[0, p] = jnp.where(even, 0.0, qa).astype(BF16)
        for r, rows in enumerate(sorted_rows(qa)):
            q16e_ref[0, p, r] = jnp.where(even_s, rows, 0.0).astype(BF16)
            q16o_ref[0, p, r] = jnp.where(even_s, 0.0, rows).astype(BF16)
        ka = _rope(ka_all[:, lanes], ca, sa, ROPE_DIMS // 2)
        ka_ref[0, p] = ka.astype(BF16)
        for r, rows in enumerate(sorted_rows(ka)):
            k16_ref[0, p, r] = rows.astype(BF16)
        va = va_all[:, lanes]
        va_ref[0, p] = va.astype(BF16)
        for r, rows in enumerate(sorted_rows(va)):
            v16_ref[0, p, r] = rows.astype(BF16)
    qr_all, kr_all, vr_all, gr_all = proj(3), proj(4), proj(5), proj(6)
    for p in range(N_PAIRS):
        lanes = slice(p * PAIR, (p + 1) * PAIR)
        qr = _rope(qr_all[:, lanes], cr, sr, HEAD_DIM // 2)
        qre_ref[0, p] = jnp.where(even, qr, 0.0).astype(BF16)
        qro_ref[0, p] = jnp.where(even, 0.0, qr).astype(BF16)
        kr_ref[0, p] = _rope(kr_all[:, lanes], cr, sr, HEAD_DIM // 2) * (HEAD_DIM ** -0.5)
        vr_ref[0, p] = vr_all[:, lanes].astype(BF16)
        gr_ref[0, p] = gr_all[:, lanes]


def _hyb_in(x, norm_g, w_in, tabs, layer, j, b, t):
    n, d = x.shape
    n_g = norm_g.shape[1]
    per_b = t // IN_ROWS
    nat_spec = pl.BlockSpec((1, N_PAIRS, IN_ROWS, PAIR), lambda i: (i // per_b, 0, i % per_b, 0))
    sorted_spec = pl.BlockSpec((1, N_PAIRS, DIL, IN_ROWS // DIL, PAIR),
                               lambda i: (i // per_b, 0, 0, i % per_b, 0))
    tab_spec = pl.BlockSpec((IN_ROWS, PAIR), lambda i: (i % per_b, 0))
    nat = lambda dt: jax.ShapeDtypeStruct((b, N_PAIRS, t, PAIR), dt)
    srt = jax.ShapeDtypeStruct((b, N_PAIRS, DIL, t // DIL, PAIR), BF16)
    return pl.pallas_call(
        _hyb_in_kernel,
        out_shape=[nat(BF16)] * 4 + [srt] * 4 + [nat(BF16), nat(BF16), nat(F32), nat(BF16), nat(F32)],
        grid=(n // IN_ROWS,),
        in_specs=[
            pl.BlockSpec((IN_ROWS, d), lambda i: (i, 0)),
            pl.BlockSpec((None, n_g, d), lambda i: (layer, 0, 0)),
            pl.BlockSpec((None,) + w_in.shape[1:], lambda i: (j, 0, 0)),
            tab_spec, tab_spec, tab_spec, tab_spec,
        ],
        out_specs=[nat_spec] * 4 + [sorted_spec] * 4 + [nat_spec] * 5,
        scratch_shapes=[pltpu.VMEM((IN_ROWS, PAIR), F32), pltpu.VMEM((IN_ROWS, PAIR), F32)],
        compiler_params=_params(1),
    )(x, norm_g, w_in, *tabs)


def _pair_select(top_bottom):
    lane = lax.broadcasted_iota(jnp.int32, (BLK, PAIR), 1)
    return jnp.where(lane < HEAD_DIM, top_bottom[:BLK], top_bottom[BLK:])


def _pair_bcast(col):
    lane = lax.broadcasted_iota(jnp.int32, (BLK, PAIR), 1)
    return jnp.where(lane < HEAD_DIM, col[:BLK], col[BLK:])


def _dil_kernel(qe_ref, qo_ref, k_ref, v_ref, cap_ref, mult_ref, o_ref, lse_ref):
    r4 = pl.program_id(1)
    ones = jnp.ones((4 * BLK, PAIR), BF16)
    for p in range(N_PAIRS):
        kcat = jnp.concatenate([k_ref[0, p, ap, 0] for ap in range(4)], axis=0)
        vext = jnp.concatenate(
            [jnp.concatenate([v_ref[0, p, ap, 0] for ap in range(4)], axis=0), ones], axis=1)
        for a0 in range(0, 4, DIL_GROUP):
            group = range(a0, a0 + DIL_GROUP)
            trows = slice(2 * a0 * BLK, 2 * (a0 + DIL_GROUP) * BLK)
            lhs = jnp.concatenate(
                [ref[0, p, a, 0] for a in group for ref in (qe_ref, qo_ref)], axis=0)
            s = jnp.minimum(_dot_nt(lhs, kcat), cap_ref[trows, :])
            m = jnp.max(s, axis=-1, keepdims=True)
            e = jnp.exp2(s - m)
            tile = lambda n, ap: e[2 * n * BLK:2 * (n + 1) * BLK, ap * BLK:(ap + 1) * BLK]
            pm = jnp.concatenate([
                jnp.concatenate([
                    tile(n, ap) * mult_ref[...] if ap == a else tile(n, ap)
                    for ap in range(4)], axis=1)
                for n, a in enumerate(group)], axis=0)
            acc = _dot(pm.astype(BF16), vext)
            den = acc[:, PAIR:]
            out = acc[:, :PAIR] / den
            lse = m + jnp.log(den) * LOG2_E
            for n, a in enumerate(group):
                rows = pl.ds(4 * a + r4, BLK, stride=DIL)
                o_ref[0, p, rows, :] = _pair_select(out[2 * n * BLK:2 * (n + 1) * BLK])
                lse_ref[0, p, rows, :] = _pair_select(lse[2 * n * BLK:2 * (n + 1) * BLK])


def _dil_tables():
    i = (np.arange(2 * BLK) % BLK)[:, None]
    j = np.arange(BLK)[None, :]
    cap = np.empty((4, 2 * BLK, 4, BLK), np.float32)
    for a in range(4):
        for ap in range(4):
            delta = 4 * (i - j) + (a - ap)
            in_d4 = (delta >= 0) & (delta <= BLK)
            in_d16 = (j <= i) & (a == ap)
            cap[a, :, ap, :] = np.where(in_d4 | in_d16, np.finfo(np.float32).max, NEG_INF)
    both = (i - j >= 0) & (4 * (i - j) <= BLK)
    mult = np.where(both, 2.0, 1.0).astype(np.float32)
    return jnp.asarray(cap.reshape(8 * BLK, 4 * BLK)), jnp.asarray(mult)


def _dil(qe, qo, k, v, cap, mult):
    b, _, _, nq, _ = k.shape
    t = nq * DIL
    by_res = lambda arr: arr.reshape(b, N_PAIRS, 4, 4, nq, PAIR)
    res_spec = pl.BlockSpec((1, N_PAIRS, 4, 1, nq, PAIR), lambda i, j: (i, 0, 0, j, 0, 0))
    nat_spec = pl.BlockSpec((1, N_PAIRS, t, PAIR), lambda i, j: (i, 0, 0, 0))
    nat_shape = jax.ShapeDtypeStruct((b, N_PAIRS, t, PAIR), F32)
    return pl.pallas_call(
        _dil_kernel,
        out_shape=[nat_shape, nat_shape],
        grid=(b, 4),
        in_specs=[res_spec, res_spec, res_spec, res_spec,
                  pl.BlockSpec(cap.shape, lambda i, j: (0, 0)),
                  pl.BlockSpec(mult.shape, lambda i, j: (0, 0))],
        out_specs=[nat_spec, nat_spec],
        compiler_params=_params(2),
    )(by_res(qe), by_res(qo), by_res(k), by_res(v), cap, mult)


def _hyb_main_kernel(x_ref, g_ref, qe_ref, qo_ref, k_ref, v_ref, o23_ref, l23_ref,
                     qre_ref, qro_ref, kr_ref, vr_ref, gr_ref,
                     cap_ref, decay_ref, zeta_ref, xi_ref, cd_ref, wout_ref,
                     o_ref, state_ref, m_scr):
    step = pl.program_id(1)

    @pl.when(step == 0)
    def _():
        state_ref[...] = jnp.zeros(state_ref.shape, F32)

    pairs = range(N_PAIRS)
    stack = lambda vals: jnp.concatenate(vals, axis=0)
    piece = lambda val, p, n: val[p * n:(p + 1) * n]
    lane = lax.broadcasted_iota(jnp.int32, (N_PAIRS * BLK, PAIR), 1)
    even = lane < HEAD_DIM
    row = lax.broadcasted_iota(jnp.int32, (N_PAIRS * BLK, PAIR), 0)
    same_head = lax.shift_right_logical(row & (PAIR - 1), 6) == lax.shift_right_logical(lane, 6)
    ones = jnp.ones((2 * BLK, PAIR), BF16)
    for c in range(MAIN_ROWS // BLK):
        rows = slice(c * BLK, (c + 1) * BLK)
        nb = step * (MAIN_ROWS // BLK) + c
        kstart = pl.multiple_of(jnp.maximum(nb - 1, 0) * BLK, BLK)
        cap = cap_ref[jnp.minimum(nb, 1)]
        s = stack([
            _dot_nt(jnp.concatenate([qe_ref[0, p, rows, :], qo_ref[0, p, rows, :]], axis=0),
                    k_ref[0, p, pl.ds(kstart, 2 * BLK), :]) for p in pairs])
        s = jnp.minimum(s, cap)
        m1 = jnp.max(s, axis=-1, keepdims=True)
        e = jnp.exp2(s - m1).astype(BF16)
        acc = [_dot(piece(e, p, 2 * BLK),
                    jnp.concatenate([v_ref[0, p, pl.ds(kstart, 2 * BLK), :], ones], axis=1))
               for p in pairs]
        num1 = stack([_pair_select(acc[p][:, :PAIR]) for p in pairs])
        den1 = stack([_pair_select(acc[p][:, PAIR:]) for p in pairs])
        m1 = stack([_pair_bcast(piece(m1, p, 2 * BLK)) for p in pairs])
        l23 = stack([l23_ref[0, p, rows, :] for p in pairs])
        o23 = stack([o23_ref[0, p, rows, :] for p in pairs])
        top = jnp.maximum(l23, m1)
        w23 = jnp.exp2(l23 - top)
        w1 = jnp.exp2(m1 - top)
        attn = ((o23 * w23 + num1 * w1) / (w23 + den1 * w1)).astype(BF16)
        qst = [jnp.concatenate([qre_ref[0, p, rows, :], qro_ref[0, p, rows, :]], axis=0)
               for p in pairs]
        kr = stack([kr_ref[0, p, rows, :] for p in pairs])
        vr = [vr_ref[0, p, rows, :] for p in pairs]
        kr16 = kr.astype(BF16)
        scores = stack([_dot_nt(qst[p], piece(kr16, p, BLK)) for p in pairs]) * decay_ref[...]
        scores = scores.astype(BF16)
        inner = stack([_pair_select(_dot(piece(scores, p, 2 * BLK), vr[p])) for p in pairs])
        state = state_ref[...]
        state16 = state.astype(BF16)
        cross2 = [_dot(qst[p], piece(state16, p, PAIR)) for p in pairs]
        cross = stack([c2[:BLK] + c2[BLK:] for c2 in cross2]) * xi_ref[...]
        kz = kr * zeta_ref[...]
        kv = stack([_dot(piece(kz, p, BLK).T.astype(BF16), vr[p]) for p in pairs])
        state_ref[...] = state * cd_ref[...] + jnp.where(same_head, kv, 0.0)
        out = inner + cross
        sq = out * out
        ms_e = jnp.sum(jnp.where(even, sq, 0.0), axis=-1, keepdims=True)
        ms_o = jnp.sum(jnp.where(even, 0.0, sq), axis=-1, keepdims=True)
        ms = jnp.where(even, ms_e, ms_o) * (1.0 / HEAD_DIM)
        out = out * lax.rsqrt(ms + EPS)
        gate = stack([gr_ref[0, p, rows, :] for p in pairs])
        ret = ((gate * jax.nn.sigmoid(gate)) * out).astype(BF16)
        for p in pairs:
            m_scr[rows, p * PAIR:(p + 1) * PAIR] = piece(attn, p, BLK)
            m_scr[rows, MIX_W + p * PAIR:MIX_W + (p + 1) * PAIR] = piece(ret, p, BLK)
    y = _dot(m_scr[...], wout_ref[...])
    o_ref[0] = x_ref[0] + _rms(y, g_ref[3:4, :])


def _window_cap():
    i = (np.arange(2 * BLK) % BLK)[:, None]
    j = np.arange(2 * BLK)[None, :]
    fmax = np.finfo(np.float32).max
    delta = np.stack([i - j, BLK + i - j])
    cap = np.where((delta >= 0) & (delta <= BLK), fmax, NEG_INF).astype(np.float32)
    return jnp.asarray(np.tile(cap, (1, N_PAIRS, 1)))


def _hyb_main(x, norm_g, qe, qo, k, v, o23, l23, qre, qro, kr, vr, gr, cap, consts, w_out,
              layer, j):
    b, t, d = x.shape
    n_g = norm_g.shape[1]
    blk = pl.BlockSpec((1, N_PAIRS, MAIN_ROWS, PAIR), lambda i, s: (i, 0, s, 0))
    full = pl.BlockSpec((1, N_PAIRS, t, PAIR), lambda i, s: (i, 0, 0, 0))
    xblk = pl.BlockSpec((1, MAIN_ROWS, d), lambda i, s: (i, s, 0))
    const3 = lambda arr: pl.BlockSpec(arr.shape, lambda i, s: (0,) * arr.ndim)
    decay, zeta, xi, cd = consts
    return pl.pallas_call(
        _hyb_main_kernel,
        out_shape=jax.ShapeDtypeStruct((b, t, d), F32),
        grid=(b, t // MAIN_ROWS),
        in_specs=[
            xblk,
            pl.BlockSpec((None, n_g, d), lambda i, s: (layer, 0, 0)),
            blk, blk, full, full, blk, blk,
            blk, blk, blk, blk, blk,
            const3(cap), const3(decay), const3(zeta), const3(xi), const3(cd),
            pl.BlockSpec((None, d, d), lambda i, s: (j, 0, 0)),
        ],
        out_specs=xblk,
        scratch_shapes=[pltpu.VMEM((N_PAIRS * PAIR, PAIR), F32),
                        pltpu.VMEM((MAIN_ROWS, d), BF16)],
        compiler_params=_params(2),
    )(x, norm_g, qe, qo, k, v, o23, l23, qre, qro, kr, vr, gr, cap, decay, zeta, xi, cd, w_out)


def _rope_table(t, rot_dims, theta):
    half = rot_dims // 2
    inv = theta ** (-(jnp.arange(half, dtype=F32) * 2.0 / rot_dims))
    ang = jnp.arange(t, dtype=F32)[:, None] * inv[None, :]
    cos, sin = jnp.cos(ang), jnp.sin(ang)
    rest = HEAD_DIM - rot_dims
    cos64 = jnp.concatenate([cos, cos, jnp.ones((t, rest), F32)], axis=-1)
    sin64 = jnp.concatenate([-sin, sin, jnp.zeros((t, rest), F32)], axis=-1)
    return jnp.tile(cos64, (1, 2)), jnp.tile(sin64, (1, 2))


def _retention_consts(n_heads):
    c = BLK
    log_g = jnp.log(1.0 - jnp.exp2(-5.0 - jnp.arange(n_heads, dtype=F32)))
    idx = jnp.arange(c, dtype=F32)
    diff = idx[:, None] - idx[None, :]
    decay = jnp.where(diff >= 0, jnp.exp(log_g[:, None, None] * jnp.maximum(diff, 0.0)), 0.0)
    zeta = jnp.exp(log_g[:, None] * (c - 1.0 - idx)[None, :])
    xi = jnp.exp(log_g[:, None] * (idx + 1.0)[None, :])
    chunk_decay = jnp.exp(log_g * c)
    per_lane = lambda hc: jnp.repeat(
        hc.reshape(N_PAIRS, 2, -1).transpose(0, 2, 1), HEAD_DIM, axis=-1).reshape(-1, PAIR)
    decay_st = decay.reshape(2 * N_PAIRS * c, c)
    chunk_decay = jnp.broadcast_to(chunk_decay[:, None], (n_heads, PAIR))
    return decay_st, per_lane(zeta), per_lane(xi), per_lane(chunk_decay)


def kernel(x, norm_g, ffn_w_gate, ffn_w_up, ffn_w_down, hyb_w_in, hyb_w_out,
           gmlp_w_in, gmlp_ln_g, gmlp_ln_b, gmlp_w_s, gmlp_b_s, gmlp_w_out):
    b, t, d = x.shape
    depth = norm_g.shape[0]
    wg, wu, wd = (w.astype(BF16) for w in (ffn_w_gate, ffn_w_up, ffn_w_down))
    hyb_in, hyb_out = hyb_w_in.astype(BF16), hyb_w_out.astype(BF16)
    g_in, g_out = gmlp_w_in.astype(BF16), gmlp_w_out.astype(BF16)
    tabs = _rope_table(t, ROPE_DIMS, ROPE_THETA) + _rope_table(t, HEAD_DIM, RET_ROPE_THETA)
    consts = _retention_consts(2 * N_PAIRS)
    dil_cap, dil_mult = _dil_tables()
    win_cap = _window_cap()
    bias_full = jnp.repeat(jnp.swapaxes(gmlp_b_s, 1, 2), BLK, axis=2)
    ln_g, ln_b = gmlp_ln_g[:, None, :], gmlp_ln_b[:, None, :]
    x = x.reshape(b * t, d)
    for layer in range(depth):
        j = layer // 2
        x = _ffn(x, norm_g, wg, wu, wd, layer, 0)
        if layer % 2 == 0:
            (qae, qao, ka, va, q16e, q16o, k16, v16, qre, qro, kr, vr, gr) = _hyb_in(
                x, norm_g, hyb_in, tabs, layer, j, b, t)
            o23, l23 = _dil(q16e, q16o, k16, v16, dil_cap, dil_mult)
            x = _hyb_main(x.reshape(b, t, d), norm_g, qae, qao, ka, va, o23, l23,
                          qre, qro, kr, vr, gr, win_cap, consts, hyb_out, layer, j
                          ).reshape(b * t, d)
        else:
            x = _gmlp(x, norm_g, g_in, ln_g, ln_b, gmlp_w_s, bias_full, g_out, layer, j)
        x = _ffn(x, norm_g, wg, wu, wd, layer, 1)
    return x.reshape(b, t, d)
```

```python
import functools

import jax
import jax.numpy as jnp
import numpy as np
from jax import lax
from jax.experimental import pallas as pl
from jax.experimental.pallas import tpu as pltpu

F32 = jnp.float32
BF16 = jnp.bfloat16

D_MODEL = 1024
HEAD_DIM = 64
PAIR = 2 * HEAD_DIM
N_PAIRS = 4
MIX_W = N_PAIRS * PAIR
BLK = 128
DIL = 16
D_FF = 2816
FF_CHUNK = 256
GMLP_GROUPS = 8
ROPE_THETA = 500000.0
ROPE_DIMS = HEAD_DIM // 4
RET_ROPE_THETA = 10000.0
EPS = 1e-6
NEG_INF = -1e30
LOG2_E = np.float32(np.log2(np.e))
QK_SCALE = np.float32(HEAD_DIM ** -0.5 * np.log2(np.e))

FFN_ROWS = 512
FFN_NORM_GROUPS = 8
GMLP_ROWS = 512
IN_ROWS = 512
MAIN_ROWS = 512
DIL_GROUP = 2
VMEM_LIMIT = 56 * 1024 * 1024


def _dot(a, b):
    return jnp.dot(a, b, preferred_element_type=F32)


def _dot_nt(a, b):
    return lax.dot_general(a, b, (((1,), (1,)), ((), ())), preferred_element_type=F32)


def _rms(x, g):
    y = x * lax.rsqrt(jnp.mean(x * x, axis=-1, keepdims=True) + EPS)
    return y * g


def _gelu(x):
    return 0.5 * x * (1.0 + lax.erf(x * np.float32(np.sqrt(0.5))))


def _params(n_axes):
    return pltpu.CompilerParams(
        dimension_semantics=("arbitrary",) * n_axes, vmem_limit_bytes=VMEM_LIMIT)


def _zero_from(*vals):
    flags = None
    for v in vals:
        for r in range(0, v.shape[0], 8):
            for l in range(0, v.shape[1], PAIR):
                f = jnp.where(v[r:r + 8, l:l + PAIR] > 0.0, 1, 0)
                flags = f if flags is None else flags | f
    zero = lax.shift_right_logical(flags, 1).astype(F32)[0:1, :]
    return jnp.concatenate([zero] * (FF_CHUNK // PAIR), axis=1)


def _ffn_kernel(xprev_ref, xnext_ref, g_ref, wg_ref, wu_ref, wd_ref, o_ref, h_scr, acc_scr,
                *, g_row, n_tiles):
    i = pl.program_id(0)
    g_in = g_ref[g_row:g_row + 1, :]
    g_out = g_ref[g_row + 1:g_row + 2, :]

    @pl.when(i == 0)
    def _():
        h_scr[...] = _rms(xprev_ref[...], g_in).astype(BF16)
        acc_scr[...] = jnp.zeros(acc_scr.shape, F32)

    @pl.when(i < n_tiles)
    def _():
        h = h_scr[...]
        zero_rows = []
        group = FFN_ROWS // FFN_NORM_GROUPS
        for j in range(FFN_NORM_GROUPS):
            rs = slice(j * group, (j + 1) * group)
            h_next = _rms(xnext_ref[rs, :], g_in)
            h_scr[rs, :] = h_next.astype(BF16)
            out_prev = xprev_ref[rs, :] + 0.5 * _rms(acc_scr[rs, :], g_out)
            o_ref[rs, :] = out_prev
            zero_rows.append(_zero_from(h_next, out_prev))
        acc = jnp.zeros(acc_scr.shape, F32)
        for c in range(D_FF // FF_CHUNK):
            sl = slice(c * FF_CHUNK, (c + 1) * FF_CHUNK)
            gate = _dot(h, wg_ref[:, sl].astype(BF16))
            up = _dot(h, wu_ref[:, sl].astype(BF16))
            if 1 <= c <= FFN_NORM_GROUPS:
                up = up + zero_rows[c - 1]
            act = (gate * jax.nn.sigmoid(gate)) * up
            acc = acc + _dot(act.astype(BF16), wd_ref[sl, :].astype(BF16))
        acc_scr[...] = acc

    @pl.when(i == n_tiles)
    def _():
        o_ref[...] = xprev_ref[...] + 0.5 * _rms(acc_scr[...], g_out)


def _ffn(x, norm_g, wg, wu, wd, layer, which):
    n, d = x.shape
    n_g = norm_g.shape[1]
    n_tiles = n // FFN_ROWS
    pick = lambda i: (layer, which, 0, 0)
    resident = dict(pipeline_mode=pl.Buffered(1))
    prev_tile = lambda i: (jnp.maximum(i - 1, 0), 0)
    next_tile = lambda i: (jnp.minimum(i + 1, n_tiles - 1), 0)
    return pl.pallas_call(
        functools.partial(_ffn_kernel, g_row=4 * which, n_tiles=n_tiles),
        out_shape=jax.ShapeDtypeStruct((n, d), F32),
        grid=(n_tiles + 1,),
        in_specs=[
            pl.BlockSpec((FFN_ROWS, d), prev_tile),
            pl.BlockSpec((FFN_ROWS, d), next_tile),
            pl.BlockSpec((None, n_g, d), lambda i: (layer, 0, 0)),
            pl.BlockSpec((None, None, d, D_FF), pick, **resident),
            pl.BlockSpec((None, None, d, D_FF), pick, **resident),
            pl.BlockSpec((None, None, D_FF, d), pick, **resident),
        ],
        out_specs=pl.BlockSpec((FFN_ROWS, d), prev_tile),
        scratch_shapes=[pltpu.VMEM((FFN_ROWS, d), BF16), pltpu.VMEM((FFN_ROWS, d), F32)],
        compiler_params=_params(1),
    )(x, x, norm_g, wg, wu, wd)


def _gmlp_kernel(x_ref, g_ref, win_ref, lng_ref, lnb_ref, ws_ref, bias_ref, wout_ref,
                 o_ref, m_scr):
    x = x_ref[...]
    rows, d = x.shape
    h = _rms(x, g_ref[2:3, :]).astype(BF16)
    u = _gelu(_dot(h, win_ref[:, :d]))
    v = _gelu(_dot(h, win_ref[:, d:]))
    mu = jnp.mean(v, axis=-1, keepdims=True)
    var = jnp.mean(jnp.square(v - mu), axis=-1, keepdims=True)
    v = ((v - mu) * lax.rsqrt(var + EPS)) * lng_ref[...] + lnb_ref[...]
    ii = lax.broadcasted_iota(jnp.int32, (BLK, BLK), 0)
    jj = lax.broadcasted_iota(jnp.int32, (BLK, BLK), 1)
    causal = jj <= ii
    for grp in range(GMLP_GROUPS):
        lanes = slice(grp * BLK, (grp + 1) * BLK)
        w = jnp.where(causal, ws_ref[grp], 0.0).astype(BF16)
        for c in range(rows // BLK):
            rs = slice(c * BLK, (c + 1) * BLK)
            s = _dot(w, v[rs, lanes].astype(BF16)) + bias_ref[:, lanes]
            m_scr[rs, lanes] = (u[rs, lanes] * s).astype(BF16)
    y = _dot(m_scr[...], wout_ref[...])
    o_ref[...] = x + _rms(y, g_ref[3:4, :])


def _gmlp(x, norm_g, w_in, ln_g, ln_b, w_s, bias_full, w_out, layer, j):
    n, d = x.shape
    n_g = norm_g.shape[1]
    sel = lambda i: (j, 0, 0)
    return pl.pallas_call(
        _gmlp_kernel,
        out_shape=jax.ShapeDtypeStruct((n, d), F32),
        grid=(n // GMLP_ROWS,),
        in_specs=[
            pl.BlockSpec((GMLP_ROWS, d), lambda i: (i, 0)),
            pl.BlockSpec((None, n_g, d), lambda i: (layer, 0, 0)),
            pl.BlockSpec((None, d, 2 * d), sel),
            pl.BlockSpec((None, 1, d), sel),
            pl.BlockSpec((None, 1, d), sel),
            pl.BlockSpec((None, GMLP_GROUPS, BLK, BLK), lambda i: (j, 0, 0, 0)),
            pl.BlockSpec((None, BLK, d), sel),
            pl.BlockSpec((None, d, d), sel),
        ],
        out_specs=pl.BlockSpec((GMLP_ROWS, d), lambda i: (i, 0)),
        scratch_shapes=[pltpu.VMEM((GMLP_ROWS, d), BF16)],
        compiler_params=_params(1),
    )(x, norm_g, w_in, ln_g, ln_b, w_s, bias_full, w_out)


def _rope(z, cos, sin, half):
    lane = lax.broadcasted_iota(jnp.int32, z.shape, 1)
    up = pltpu.roll(z, PAIR - half, axis=1)
    dn = pltpu.roll(z, half, axis=1)
    partner = jnp.where((lane & (HEAD_DIM - 1)) < half, up, dn)
    return z * cos + partner * sin


def _hyb_in_kernel(x_ref, g_ref, w_ref, ca_ref, sa_ref, cr_ref, sr_ref,
                   qae_ref, qao_ref, ka_ref, va_ref,
                   q16e_ref, q16o_ref, k16_ref, v16_ref,
                   qre_ref, qro_ref, kr_ref, vr_ref, gr_ref, sort_scr, sort2_scr):
    h = _rms(x_ref[...], g_ref[2:3, :]).astype(BF16)
    ca, sa, cr, sr = ca_ref[...], sa_ref[...], cr_ref[...], sr_ref[...]
    even = lax.broadcasted_iota(jnp.int32, (IN_ROWS, PAIR), 1) < HEAD_DIM
    even_s = lax.broadcasted_iota(jnp.int32, (IN_ROWS // DIL, PAIR), 1) < HEAD_DIM

    def proj(group):
        return _dot(h, w_ref[:, group * MIX_W:(group + 1) * MIX_W])

    def sorted_rows(val):
        sort_scr[...] = val
        quarter = IN_ROWS // 4
        for r4 in range(4):
            sort2_scr[r4 * quarter:(r4 + 1) * quarter, :] = sort_scr[pl.ds(r4, quarter, stride=4), :]
        return [sort2_scr[pl.ds((r % 4) * quarter + r // 4, IN_ROWS // DIL, stride=4), :]
                for r in range(DIL)]

    qa_all, ka_all, va_all = proj(0), proj(1), proj(2)
    for p in range(N_PAIRS):
        lanes = slice(p * PAIR, (p + 1) * PAIR)
        qa = _rope(qa_all[:, lanes], ca, sa, ROPE_DIMS // 2) * QK_SCALE
        qae_ref[0, p] = jnp.where(even, qa, 0.0).astype(BF16)
        qao_ref[0, p] = jnp.where(even, 0.0, qa).astype(BF16)
        for r, rows in enumerate(sorted_rows(qa)):
            q16e_ref[0, p, r] = jnp.where(even_s, rows, 0.0).astype(BF16)
            q16o_ref[0, p, r] = jnp.where(even_s, 0.0, rows).astype(BF16)
        ka = _rope(ka_all[:, lanes], ca, sa, ROPE_DIMS // 2)
        ka_ref[0, p] = ka.astype(BF16)
        for r, rows in enumerate(sorted_rows(ka)):
            k16_ref[0, p, r] = rows.astype(BF16)
        va = va_all[:, lanes]
        va_ref[0, p] = va.astype(BF16)
        for r, rows in enumerate(sorted_rows(va)):
            v16_ref[0, p, r] = rows.astype(BF16)
    qr_all, kr_all, vr_all, gr_all = proj(3), proj(4), proj(5), proj(6)
    for p in range(N_PAIRS):
        lanes = slice(p * PAIR, (p + 1) * PAIR)
        qr = _rope(qr_all[:, lanes], cr, sr, HEAD_DIM // 2)
        qre_ref[0, p] = jnp.where(even, qr, 0.0).astype(BF16)
        qro_ref[0, p] = jnp.where(even, 0.0, qr).astype(BF16)
        kr_ref[0, p] = _rope(kr_all[:, lanes], cr, sr, HEAD_DIM // 2) * (HEAD_DIM ** -0.5)
        vr_ref[0, p] = vr_all[:, lanes].astype(BF16)
        gr_ref[0, p] = gr_all[:, lanes]


def _hyb_in(x, norm_g, w_in, tabs, layer, j, b, t):
    n, d = x.shape
    n_g = norm_g.shape[1]
    per_b = t // IN_ROWS
    nat_spec = pl.BlockSpec((1, N_PAIRS, IN_ROWS, PAIR), lambda i: (i // per_b, 0, i % per_b, 0))
    sorted_spec = pl.BlockSpec((1, N_PAIRS, DIL, IN_ROWS // DIL, PAIR),
                               lambda i: (i // per_b, 0, 0, i % per_b, 0))
    tab_spec = pl.BlockSpec((IN_ROWS, PAIR), lambda i: (i % per_b, 0))
    nat = lambda dt: jax.ShapeDtypeStruct((b, N_PAIRS, t, PAIR), dt)
    srt = jax.ShapeDtypeStruct((b, N_PAIRS, DIL, t // DIL, PAIR), BF16)
    return pl.pallas_call(
        _hyb_in_kernel,
        out_shape=[nat(BF16)] * 4 + [srt] * 4 + [nat(BF16), nat(BF16), nat(F32), nat(BF16), nat(F32)],
        grid=(n // IN_ROWS,),
        in_specs=[
            pl.BlockSpec((IN_ROWS, d), lambda i: (i, 0)),
            pl.BlockSpec((None, n_g, d), lambda i: (layer, 0, 0)),
            pl.BlockSpec((None,) + w_in.shape[1:], lambda i: (j, 0, 0)),
            tab_spec, tab_spec, tab_spec, tab_spec,
        ],
        out_specs=[nat_spec] * 4 + [sorted_spec] * 4 + [nat_spec] * 5,
        scratch_shapes=[pltpu.VMEM((IN_ROWS, PAIR), F32), pltpu.VMEM((IN_ROWS, PAIR), F32)],
        compiler_params=_params(1),
    )(x, norm_g, w_in, *tabs)


def _pair_select(top_bottom):
    lane = lax.broadcasted_iota(jnp.int32, (BLK, PAIR), 1)
    return jnp.where(lane < HEAD_DIM, top_bottom[:BLK], top_bottom[BLK:])


def _pair_bcast(col):
    lane = lax.broadcasted_iota(jnp.int32, (BLK, PAIR), 1)
    return jnp.where(lane < HEAD_DIM, col[:BLK], col[BLK:])


def _dil_kernel(qe_ref, qo_ref, k_ref, v_ref, cap_ref, mult_ref, o_ref, lse_ref):
    r4 = pl.program_id(1)
    ones = jnp.ones((4 * BLK, PAIR), BF16)
    for p in range(N_PAIRS):
        kcat = jnp.concatenate([k_ref[0, p, ap, 0] for ap in range(4)], axis=0)
        vext = jnp.concatenate(
            [jnp.concatenate([v_ref[0, p, ap, 0] for ap in range(4)], axis=0), ones], axis=1)
        for a0 in range(0, 4, DIL_GROUP):
            group = range(a0, a0 + DIL_GROUP)
            trows = slice(2 * a0 * BLK, 2 * (a0 + DIL_GROUP) * BLK)
            lhs = jnp.concatenate(
                [ref[0, p, a, 0] for a in group for ref in (qe_ref, qo_ref)], axis=0)
            s = jnp.minimum(_dot_nt(lhs, kcat), cap_ref[trows, :])
            m = jnp.max(s, axis=-1, keepdims=True)
            e = jnp.exp2(s - m)
            tile = lambda n, ap: e[2 * n * BLK:2 * (n + 1) * BLK, ap * BLK:(ap + 1) * BLK]
            pm = jnp.concatenate([
                jnp.concatenate([
                    tile(n, ap) * mult_ref[...] if ap == a else tile(n, ap)
                    for ap in range(4)], axis=1)
                for n, a in enumerate(group)], axis=0)
            acc = _dot(pm.astype(BF16), vext)
            den = acc[:, PAIR:]
            out = acc[:, :PAIR] / den
            lse = m + jnp.log(den) * LOG2_E
            for n, a in enumerate(group):
                rows = pl.ds(4 * a + r4, BLK, stride=DIL)
                o_ref[0, p, rows, :] = _pair_select(out[2 * n * BLK:2 * (n + 1) * BLK])
                lse_ref[0, p, rows, :] = _pair_select(lse[2 * n * BLK:2 * (n + 1) * BLK])


def _dil_tables():
    i = (np.arange(2 * BLK) % BLK)[:, None]
    j = np.arange(BLK)[None, :]
    cap = np.empty((4, 2 * BLK, 4, BLK), np.float32)
    for a in range(4):
        for ap in range(4):
            delta = 4 * (i - j) + (a - ap)
            in_d4 = (delta >= 0) & (delta <= BLK)
            in_d16 = (j <= i) & (a == ap)
            cap[a, :, ap, :] = np.where(in_d4 | in_d16, np.finfo(np.float32).max, NEG_INF)
    both = (i - j >= 0) & (4 * (i - j) <= BLK)
    mult = np.where(both, 2.0, 1.0).astype(np.float32)
    return jnp.asarray(cap.reshape(8 * BLK, 4 * BLK)), jnp.asarray(mult)


def _dil(qe, qo, k, v, cap, mult):
    b, _, _, nq, _ = k.shape
    t = nq * DIL
    by_res = lambda arr: arr.reshape(b, N_PAIRS, 4, 4, nq, PAIR)
    res_spec = pl.BlockSpec((1, N_PAIRS, 4, 1, nq, PAIR), lambda i, j: (i, 0, 0, j, 0, 0))
    nat_spec = pl.BlockSpec((1, N_PAIRS, t, PAIR), lambda i, j: (i, 0, 0, 0))
    nat_shape = jax.ShapeDtypeStruct((b, N_PAIRS, t, PAIR), F32)
    return pl.pallas_call(
        _dil_kernel,
        out_shape=[nat_shape, nat_shape],
        grid=(b, 4),
        in_specs=[res_spec, res_spec, res_spec, res_spec,
                  pl.BlockSpec(cap.shape, lambda i, j: (0, 0)),
                  pl.BlockSpec(mult.shape, lambda i, j: (0, 0))],
        out_specs=[nat_spec, nat_spec],
        compiler_params=_params(2),
    )(by_res(qe), by_res(qo), by_res(k), by_res(v), cap, mult)


def _hyb_main_kernel(x_ref, g_ref, qe_ref, qo_ref, k_ref, v_ref, o23_ref, l23_ref,
                     qre_ref, qro_ref, kr_ref, vr_ref, gr_ref,
                     cap_ref, decay_ref, zeta_ref, xi_ref, cd_ref, wout_ref,
                     o_ref, state_ref, m_scr):
    step = pl.program_id(1)

    @pl.when(step == 0)
    def _():
        state_ref[...] = jnp.zeros(state_ref.shape, F32)

    pairs = range(N_PAIRS)
    stack = lambda vals: jnp.concatenate(vals, axis=0)
    piece = lambda val, p, n: val[p * n:(p + 1) * n]
    lane = lax.broadcasted_iota(jnp.int32, (N_PAIRS * BLK, PAIR), 1)
    even = lane < HEAD_DIM
    row = lax.broadcasted_iota(jnp.int32, (N_PAIRS * BLK, PAIR), 0)
    same_head = lax.shift_right_logical(row & (PAIR - 1), 6) == lax.shift_right_logical(lane, 6)
    ones = jnp.ones((2 * BLK, PAIR), BF16)
    for c in range(MAIN_ROWS // BLK):
        rows = slice(c * BLK, (c + 1) * BLK)
        nb = step * (MAIN_ROWS // BLK) + c
        kstart = pl.multiple_of(jnp.maximum(nb - 1, 0) * BLK, BLK)
        cap = cap_ref[jnp.minimum(nb, 1)]
        s = stack([
            _dot_nt(jnp.concatenate([qe_ref[0, p, rows, :], qo_ref[0, p, rows, :]], axis=0),
                    k_ref[0, p, pl.ds(kstart, 2 * BLK), :]) for p in pairs])
        s = jnp.minimum(s, cap)
        m1 = jnp.max(s, axis=-1, keepdims=True)
        e = jnp.exp2(s - m1).astype(BF16)
        acc = [_dot(piece(e, p, 2 * BLK),
                    jnp.concatenate([v_ref[0, p, pl.ds(kstart, 2 * BLK), :], ones], axis=1))
               for p in pairs]
        num1 = stack([_pair_select(acc[p][:, :PAIR]) for p in pairs])
        den1 = stack([_pair_select(acc[p][:, PAIR:]) for p in pairs])
        m1 = stack([_pair_bcast(piece(m1, p, 2 * BLK)) for p in pairs])
        l23 = stack([l23_ref[0, p, rows, :] for p in pairs])
        o23 = stack([o23_ref[0, p, rows, :] for p in pairs])
        top = jnp.maximum(l23, m1)
        w23 = jnp.exp2(l23 - top)
        w1 = jnp.exp2(m1 - top)
        attn = ((o23 * w23 + num1 * w1) / (w23 + den1 * w1)).astype(BF16)
        qst = [jnp.concatenate([qre_ref[0, p, rows, :], qro_ref[0, p, rows, :]], axis=0)
               for p in pairs]
        kr = stack([kr_ref[0, p, rows, :] for p in pairs])
        vr = [vr_ref[0, p, rows, :] for p in pairs]
        kr16 = kr.astype(BF16)
        scores = stack([_dot_nt(qst[p], piece(kr16, p, BLK)) for p in pairs]) * decay_ref[...]
        scores = scores.astype(BF16)
        inner = stack([_pair_select(_dot(piece(scores, p, 2 * BLK), vr[p])) for p in pairs])
        state = state_ref[...]
        state16 = state.astype(BF16)
        cross2 = [_dot(qst[p], piece(state16, p, PAIR)) for p in pairs]
        cross = stack([c2[:BLK] + c2[BLK:] for c2 in cross2]) * xi_ref[...]
        kz = kr * zeta_ref[...]
        kv = stack([_dot(piece(kz, p, BLK).T.astype(BF16), vr[p]) for p in pairs])
        state_ref[...] = state * cd_ref[...] + jnp.where(same_head, kv, 0.0)
        out = inner + cross
        sq = out * out
        ms_e = jnp.sum(jnp.where(even, sq, 0.0), axis=-1, keepdims=True)
        ms_o = jnp.sum(jnp.where(even, 0.0, sq), axis=-1, keepdims=True)
        ms = jnp.where(even, ms_e, ms_o) * (1.0 / HEAD_DIM)
        out = out * lax.rsqrt(ms + EPS)
        gate = stack([gr_ref[0, p, rows, :] for p in pairs])
        ret = ((gate * jax.nn.sigmoid(gate)) * out).astype(BF16)
        for p in pairs:
            m_scr[rows, p * PAIR:(p + 1) * PAIR] = piece(attn, p, BLK)
            m_scr[rows, MIX_W + p * PAIR:MIX_W + (p + 1) * PAIR] = piece(ret, p, BLK)
    y = _dot(m_scr[...], wout_ref[...])
    o_ref[0] = x_ref[0] + _rms(y, g_ref[3:4, :])


def _window_cap():
    i = (np.arange(2 * BLK) % BLK)[:, None]
    j = np.arange(2 * BLK)[None, :]
    fmax = np.finfo(np.float32).max
    delta = np.stack([i - j, BLK + i - j])
    cap = np.where((delta >= 0) & (delta <= BLK), fmax, NEG_INF).astype(np.float32)
    return jnp.asarray(np.tile(cap, (1, N_PAIRS, 1)))


def _hyb_main(x, norm_g, qe, qo, k, v, o23, l23, qre, qro, kr, vr, gr, cap, consts, w_out,
              layer, j):
    b, t, d = x.shape
    n_g = norm_g.shape[1]
    blk = pl.BlockSpec((1, N_PAIRS, MAIN_ROWS, PAIR), lambda i, s: (i, 0, s, 0))
    full = pl.BlockSpec((1, N_PAIRS, t, PAIR), lambda i, s: (i, 0, 0, 0))
    xblk = pl.BlockSpec((1, MAIN_ROWS, d), lambda i, s: (i, s, 0))
    const3 = lambda arr: pl.BlockSpec(arr.shape, lambda i, s: (0,) * arr.ndim)
    decay, zeta, xi, cd = consts
    return pl.pallas_call(
        _hyb_main_kernel,
        out_shape=jax.ShapeDtypeStruct((b, t, d), F32),
        grid=(b, t // MAIN_ROWS),
        in_specs=[
            xblk,
            pl.BlockSpec((None, n_g, d), lambda i, s: (layer, 0, 0)),
            blk, blk, full, full, blk, blk,
            blk, blk, blk, blk, blk,
            const3(cap), const3(decay), const3(zeta), const3(xi), const3(cd),
            pl.BlockSpec((None, d, d), lambda i, s: (j, 0, 0)),
        ],
        out_specs=xblk,
        scratch_shapes=[pltpu.VMEM((N_PAIRS * PAIR, PAIR), F32),
                        pltpu.VMEM((MAIN_ROWS, d), BF16)],
        compiler_params=_params(2),
    )(x, norm_g, qe, qo, k, v, o23, l23, qre, qro, kr, vr, gr, cap, decay, zeta, xi, cd, w_out)


def _rope_table(t, rot_dims, theta):
    half = rot_dims // 2
    inv = theta ** (-(jnp.arange(half, dtype=F32) * 2.0 / rot_dims))
    ang = jnp.arange(t, dtype=F32)[:, None] * inv[None, :]
    cos, sin = jnp.cos(ang), jnp.sin(ang)
    rest = HEAD_DIM - rot_dims
    cos64 = jnp.concatenate([cos, cos, jnp.ones((t, rest), F32)], axis=-1)
    sin64 = jnp.concatenate([-sin, sin, jnp.zeros((t, rest), F32)], axis=-1)
    return jnp.tile(cos64, (1, 2)), jnp.tile(sin64, (1, 2))


def _retention_consts(n_heads):
    c = BLK
    log_g = jnp.log(1.0 - jnp.exp2(-5.0 - jnp.arange(n_heads, dtype=F32)))
    idx = jnp.arange(c, dtype=F32)
    diff = idx[:, None] - idx[None, :]
    decay = jnp.where(diff >= 0, jnp.exp(log_g[:, None, None] * jnp.maximum(diff, 0.0)), 0.0)
    zeta = jnp.exp(log_g[:, None] * (c - 1.0 - idx)[None, :])
    xi = jnp.exp(log_g[:, None] * (idx + 1.0)[None, :])
    chunk_decay = jnp.exp(log_g * c)
    per_lane = lambda hc: jnp.repeat(
        hc.reshape(N_PAIRS, 2, -1).transpose(0, 2, 1), HEAD_DIM, axis=-1).reshape(-1, PAIR)
    decay_st = decay.reshape(2 * N_PAIRS * c, c)
    chunk_decay = jnp.broadcast_to(chunk_decay[:, None], (n_heads, PAIR))
    return decay_st, per_lane(zeta), per_lane(xi), per_lane(chunk_decay)


def kernel(x, norm_g, ffn_w_gate, ffn_w_up, ffn_w_down, hyb_w_in, hyb_w_out,
           gmlp_w_in, gmlp_ln_g, gmlp_ln_b, gmlp_w_s, gmlp_b_s, gmlp_w_out):
    b, t, d = x.shape
    depth = norm_g.shape[0]
    wg, wu, wd = ffn_w_gate, ffn_w_up, ffn_w_down
    hyb_in, hyb_out = hyb_w_in.astype(BF16), hyb_w_out.astype(BF16)
    g_in, g_out = gmlp_w_in.astype(BF16), gmlp_w_out.astype(BF16)
    tabs = _rope_table(t, ROPE_DIMS, ROPE_THETA) + _rope_table(t, HEAD_DIM, RET_ROPE_THETA)
    consts = _retention_consts(2 * N_PAIRS)
    dil_cap, dil_mult = _dil_tables()
    win_cap = _window_cap()
    bias_full = jnp.repeat(jnp.swapaxes(gmlp_b_s, 1, 2), BLK, axis=2)
    ln_g, ln_b = gmlp_ln_g[:, None, :], gmlp_ln_b[:, None, :]
    x = x.reshape(b * t, d)
    for layer in range(depth):
        j = layer // 2
        x = _ffn(x, norm_g, wg, wu, wd, layer, 0)
        if layer % 2 == 0:
            (qae, qao, ka, va, q16e, q16o, k16, v16, qre, qro, kr, vr, gr) = _hyb_in(
                x, norm_g, hyb_in, tabs, layer, j, b, t)
            o23, l23 = _dil(q16e, q16o, k16, v16, dil_cap, dil_mult)
            x = _hyb_main(x.reshape(b, t, d), norm_g, qae, qao, ka, va, o23, l23,
                          qre, qro, kr, vr, gr, win_cap, consts, hyb_out, layer, j
                          ).reshape(b * t, d)
        else:
            x = _gmlp(x, norm_g, g_in, ln_g, ln_b, gmlp_w_s, bias_full, g_out, layer, j)
        x = _ffn(x, norm_g, wg, wu, wd, layer, 1)
    return x.reshape(b, t, d)
```

```python
import functools

import jax
import jax.numpy as jnp
import numpy as np
from jax import lax
from jax.experimental import pallas as pl
from jax.experimental.pallas import tpu as pltpu

F32 = jnp.float32
BF16 = jnp.bfloat16

D_MODEL = 1024
HEAD_DIM = 64
PAIR = 2 * HEAD_DIM
N_PAIRS = 4
MIX_W = N_PAIRS * PAIR
BLK = 128
DIL = 16
D_FF = 2816
FF_CHUNK = 256
GMLP_GROUPS = 8
ROPE_THETA = 500000.0
ROPE_DIMS = HEAD_DIM // 4
RET_ROPE_THETA = 10000.0
EPS = 1e-6
NEG_INF = -1e30
LOG2_E = np.float32(np.log2(np.e))
QK_SCALE = np.float32(HEAD_DIM ** -0.5 * np.log2(np.e))

FFN_ROWS = 512
FFN_NORM_GROUPS = 8
GMLP_ROWS = 512
IN_ROWS = 512
MAIN_ROWS = 512
DIL_GROUP = 2
VMEM_LIMIT = 60 * 1024 * 1024


def _dot(a, b):
    return jnp.dot(a, b, preferred_element_type=F32)


def _dot_nt(a, b):
    return lax.dot_general(a, b, (((1,), (1,)), ((), ())), preferred_element_type=F32)


def _rms(x, g):
    y = x * lax.rsqrt(jnp.mean(x * x, axis=-1, keepdims=True) + EPS)
    return y * g


def _gelu(x):
    return 0.5 * x * (1.0 + lax.erf(x * np.float32(np.sqrt(0.5))))


def _params(n_axes):
    return pltpu.CompilerParams(
        dimension_semantics=("arbitrary",) * n_axes, vmem_limit_bytes=VMEM_LIMIT)


def _zero_from(*vals):
    flags = None
    for v in vals:
        for r in range(0, v.shape[0], 8):
            for l in range(0, v.shape[1], PAIR):
                f = jnp.where(v[r:r + 8, l:l + PAIR] > 0.0, 1, 0)
                flags = f if flags is None else flags | f
    zero = lax.shift_right_logical(flags, 1).astype(F32)[0:1, :]
    return jnp.concatenate([zero] * (FF_CHUNK // PAIR), axis=1)


def _ffn_kernel(xprev_ref, xnext_ref, g_ref, wg_hbm, wu_hbm, wd_hbm, o_ref,
                h_scr, acc_scr, wg_ref, wu_ref, wd_ref, sem, *, g_row, n_tiles, layer, which):
    i = pl.program_id(0)
    g_in = g_ref[g_row:g_row + 1, :]
    g_out = g_ref[g_row + 1:g_row + 2, :]
    n_chunks = D_FF // FF_CHUNK

    def weight_copies(c):
        sl = slice(c * FF_CHUNK, (c + 1) * FF_CHUNK)
        return [
            pltpu.make_async_copy(wg_hbm.at[layer, which, :, sl], wg_ref.at[:, sl], sem.at[0, c]),
            pltpu.make_async_copy(wu_hbm.at[layer, which, :, sl], wu_ref.at[:, sl], sem.at[1, c]),
            pltpu.make_async_copy(wd_hbm.at[layer, which, sl, :], wd_ref.at[sl, :], sem.at[2, c]),
        ]

    def finish(rs):
        out_prev = xprev_ref[rs, :] + 0.5 * _rms(acc_scr[rs, :], g_out)
        o_ref[rs, :] = out_prev
        return out_prev

    def step(wait_weights):
        h = h_scr[...]
        zero_rows = []
        group = FFN_ROWS // FFN_NORM_GROUPS
        for j in range(FFN_NORM_GROUPS):
            rs = slice(j * group, (j + 1) * group)
            h_next = _rms(xnext_ref[rs, :], g_in)
            h_scr[rs, :] = h_next.astype(BF16)
            zero_rows.append(_zero_from(h_next, finish(rs)))
        acc = jnp.zeros(acc_scr.shape, F32)
        for c in range(n_chunks):
            sl = slice(c * FF_CHUNK, (c + 1) * FF_CHUNK)
            if wait_weights:
                for copy in weight_copies(c):
                    copy.wait()
            gate = _dot(h, wg_ref[:, sl].astype(BF16))
            up = _dot(h, wu_ref[:, sl].astype(BF16))
            if 1 <= c <= FFN_NORM_GROUPS:
                up = up + zero_rows[c - 1]
            act = (gate * jax.nn.sigmoid(gate)) * up
            acc = acc + _dot(act.astype(BF16), wd_ref[sl, :].astype(BF16))
        acc_scr[...] = acc

    @pl.when(i == 0)
    def _():
        for c in range(n_chunks):
            for copy in weight_copies(c):
                copy.start()
        h_scr[...] = _rms(xprev_ref[...], g_in).astype(BF16)
        acc_scr[...] = jnp.zeros(acc_scr.shape, F32)
        step(wait_weights=True)

    @pl.when(jnp.logical_and(i > 0, i < n_tiles))
    def _():
        step(wait_weights=False)

    @pl.when(i == n_tiles)
    def _():
        finish(slice(None))


def _ffn(x, norm_g, wg, wu, wd, layer, which):
    n, d = x.shape
    n_g = norm_g.shape[1]
    n_tiles = n // FFN_ROWS
    prev_tile = lambda i: (jnp.maximum(i - 1, 0), 0)
    next_tile = lambda i: (jnp.minimum(i + 1, n_tiles - 1), 0)
    in_hbm = pl.BlockSpec(memory_space=pl.ANY)
    return pl.pallas_call(
        functools.partial(_ffn_kernel, g_row=4 * which, n_tiles=n_tiles, layer=layer, which=which),
        out_shape=jax.ShapeDtypeStruct((n, d), F32),
        grid=(n_tiles + 1,),
        in_specs=[
            pl.BlockSpec((FFN_ROWS, d), prev_tile),
            pl.BlockSpec((FFN_ROWS, d), next_tile),
            pl.BlockSpec((None, n_g, d), lambda i: (layer, 0, 0)),
            in_hbm, in_hbm, in_hbm,
        ],
        out_specs=pl.BlockSpec((FFN_ROWS, d), prev_tile),
        scratch_shapes=[pltpu.VMEM((FFN_ROWS, d), BF16), pltpu.VMEM((FFN_ROWS, d), F32),
                        pltpu.VMEM((d, D_FF), F32), pltpu.VMEM((d, D_FF), F32),
                        pltpu.VMEM((D_FF, d), F32),
                        pltpu.SemaphoreType.DMA((3, D_FF // FF_CHUNK))],
        compiler_params=_params(1),
    )(x, x, norm_g, wg, wu, wd)


def _gmlp_kernel(x_ref, g_ref, win_ref, lng_ref, lnb_ref, ws_ref, bias_ref, wout_ref,
                 o_ref, m_scr):
    x = x_ref[...]
    rows, d = x.shape
    h = _rms(x, g_ref[2:3, :]).astype(BF16)
    u = _gelu(_dot(h, win_ref[:, :d]))
    v = _gelu(_dot(h, win_ref[:, d:]))
    mu = jnp.mean(v, axis=-1, keepdims=True)
    var = jnp.mean(jnp.square(v - mu), axis=-1, keepdims=True)
    v = ((v - mu) * lax.rsqrt(var + EPS)) * lng_ref[...] + lnb_ref[...]
    ii = lax.broadcasted_iota(jnp.int32, (BLK, BLK), 0)
    jj = lax.broadcasted_iota(jnp.int32, (BLK, BLK), 1)
    causal = jj <= ii
    for grp in range(GMLP_GROUPS):
        lanes = slice(grp * BLK, (grp + 1) * BLK)
        w = jnp.where(causal, ws_ref[grp], 0.0).astype(BF16)
        for c in range(rows // BLK):
            rs = slice(c * BLK, (c + 1) * BLK)
            s = _dot(w, v[rs, lanes].astype(BF16)) + bias_ref[:, lanes]
            m_scr[rs, lanes] = (u[rs, lanes] * s).astype(BF16)
    y = _dot(m_scr[...], wout_ref[...])
    o_ref[...] = x + _rms(y, g_ref[3:4, :])


def _gmlp(x, norm_g, w_in, ln_g, ln_b, w_s, bias_full, w_out, layer, j):
    n, d = x.shape
    n_g = norm_g.shape[1]
    sel = lambda i: (j, 0, 0)
    return pl.pallas_call(
        _gmlp_kernel,
        out_shape=jax.ShapeDtypeStruct((n, d), F32),
        grid=(n // GMLP_ROWS,),
        in_specs=[
            pl.BlockSpec((GMLP_ROWS, d), lambda i: (i, 0)),
            pl.BlockSpec((None, n_g, d), lambda i: (layer, 0, 0)),
            pl.BlockSpec((None, d, 2 * d), sel),
            pl.BlockSpec((None, 1, d), sel),
            pl.BlockSpec((None, 1, d), sel),
            pl.BlockSpec((None, GMLP_GROUPS, BLK, BLK), lambda i: (j, 0, 0, 0)),
            pl.BlockSpec((None, BLK, d), sel),
            pl.BlockSpec((None, d, d), sel),
        ],
        out_specs=pl.BlockSpec((GMLP_ROWS, d), lambda i: (i, 0)),
        scratch_shapes=[pltpu.VMEM((GMLP_ROWS, d), BF16)],
        compiler_params=_params(1),
    )(x, norm_g, w_in, ln_g, ln_b, w_s, bias_full, w_out)


def _rope(z, cos, sin, half):
    lane = lax.broadcasted_iota(jnp.int32, z.shape, 1)
    up = pltpu.roll(z, PAIR - half, axis=1)
    dn = pltpu.roll(z, half, axis=1)
    partner = jnp.where((lane & (HEAD_DIM - 1)) < half, up, dn)
    return z * cos + partner * sin


def _hyb_in_kernel(x_ref, g_ref, w_ref, ca_ref, sa_ref, cr_ref, sr_ref,
                   qae_ref, qao_ref, ka_ref, va_ref,
                   q16e_ref, q16o_ref, k16_ref, v16_ref,
                   qre_ref, qro_ref, kr_ref, vr_ref, gr_ref, sort_scr, sort2_scr):
    h = _rms(x_ref[...], g_ref[2:3, :]).astype(BF16)
    ca, sa, cr, sr = ca_ref[...], sa_ref[...], cr_ref[...], sr_ref[...]
    even = lax.broadcasted_iota(jnp.int32, (IN_ROWS, PAIR), 1) < HEAD_DIM
    even_s = lax.broadcasted_iota(jnp.int32, (IN_ROWS // DIL, PAIR), 1) < HEAD_DIM

    def proj(group):
        return _dot(h, w_ref[:, group * MIX_W:(group + 1) * MIX_W])

    def sorted_rows(val):
        sort_scr[...] = val
        quarter = IN_ROWS // 4
        for r4 in range(4):
            sort2_scr[r4 * quarter:(r4 + 1) * quarter, :] = sort_scr[pl.ds(r4, quarter, stride=4), :]
        return [sort2_scr[pl.ds((r % 4) * quarter + r // 4, IN_ROWS // DIL, stride=4), :]
                for r in range(DIL)]

    qa_all, ka_all, va_all = proj(0), proj(1), proj(2)
    for p in range(N_PAIRS):
        lanes = slice(p * PAIR, (p + 1) * PAIR)
        qa = _rope(qa_all[:, lanes], ca, sa, ROPE_DIMS // 2) * QK_SCALE
        qae_ref[0, p] = jnp.where(even, qa, 0.0).astype(BF16)
        qao_ref[0, p] = jnp.where(even, 0.0, qa).astype(BF16)
        for r, rows in enumerate(sorted_rows(qa)):
            q16e_ref[0, p, r] = jnp.where(even_s, rows, 0.0).astype(BF16)
            q16o_ref[0, p, r] = jnp.where(even_s, 0.0, rows).astype(BF16)
        ka = _rope(ka_all[:, lanes], ca, sa, ROPE_DIMS // 2)
        ka_ref[0, p] = ka.astype(BF16)
        for r, rows in enumerate(sorted_rows(ka)):
            k16_ref[0, p, r] = rows.astype(BF16)
        va = va_all[:, lanes]
        va_ref[0, p] = va.astype(BF16)
        for r, rows in enumerate(sorted_rows(va)):
            v16_ref[0, p, r] = rows.astype(BF16)
    qr_all, kr_all, vr_all, gr_all = proj(3), proj(4), proj(5), proj(6)
    for p in range(N_PAIRS):
        lanes = slice(p * PAIR, (p + 1) * PAIR)
        qr = _rope(qr_all[:, lanes], cr, sr, HEAD_DIM // 2)
        qre_ref[0, p] = jnp.where(even, qr, 0.0).astype(BF16)
        qro_ref[0, p] = jnp.where(even, 0.0, qr).astype(BF16)
        kr_ref[0, p] = _rope(kr_all[:, lanes], cr, sr, HEAD_DIM // 2) * (HEAD_DIM ** -0.5)
        vr_ref[0, p] = vr_all[:, lanes].astype(BF16)
        gr_ref[0, p] = gr_all[:, lanes]


def _hyb_in(x, norm_g, w_in, tabs, layer, j, b, t):
    n, d = x.shape
    n_g = norm_g.shape[1]
    per_b = t // IN_ROWS
    nat_spec = pl.BlockSpec((1, N_PAIRS, IN_ROWS, PAIR), lambda i: (i // per_b, 0, i % per_b, 0))
    sorted_spec = pl.BlockSpec((1, N_PAIRS, DIL, IN_ROWS // DIL, PAIR),
                               lambda i: (i // per_b, 0, 0, i % per_b, 0))
    tab_spec = pl.BlockSpec((IN_ROWS, PAIR), lambda i: (i % per_b, 0))
    nat = lambda dt: jax.ShapeDtypeStruct((b, N_PAIRS, t, PAIR), dt)
    srt = jax.ShapeDtypeStruct((b, N_PAIRS, DIL, t // DIL, PAIR), BF16)
    return pl.pallas_call(
        _hyb_in_kernel,
        out_shape=[nat(BF16)] * 4 + [srt] * 4 + [nat(BF16), nat(BF16), nat(F32), nat(BF16), nat(F32)],
        grid=(n // IN_ROWS,),
        in_specs=[
            pl.BlockSpec((IN_ROWS, d), lambda i: (i, 0)),
            pl.BlockSpec((None, n_g, d), lambda i: (layer, 0, 0)),
            pl.BlockSpec((None,) + w_in.shape[1:], lambda i: (j, 0, 0)),
            tab_spec, tab_spec, tab_spec, tab_spec,
        ],
        out_specs=[nat_spec] * 4 + [sorted_spec] * 4 + [nat_spec] * 5,
        scratch_shapes=[pltpu.VMEM((IN_ROWS, PAIR), F32), pltpu.VMEM((IN_ROWS, PAIR), F32)],
        compiler_params=_params(1),
    )(x, norm_g, w_in, *tabs)


def _pair_select(top_bottom):
    lane = lax.broadcasted_iota(jnp.int32, (BLK, PAIR), 1)
    return jnp.where(lane < HEAD_DIM, top_bottom[:BLK], top_bottom[BLK:])


def _pair_bcast(col):
    lane = lax.broadcasted_iota(jnp.int32, (BLK, PAIR), 1)
    return jnp.where(lane < HEAD_DIM, col[:BLK], col[BLK:])


def _dil_kernel(qe_ref, qo_ref, k_ref, v_ref, cap_ref, mult_ref, o_ref, lse_ref):
    r4 = pl.program_id(1)
    ones = jnp.ones((4 * BLK, PAIR), BF16)
    for p in range(N_PAIRS):
        kcat = jnp.concatenate([k_ref[0, p, ap, 0] for ap in range(4)], axis=0)
        vext = jnp.concatenate(
            [jnp.concatenate([v_ref[0, p, ap, 0] for ap in range(4)], axis=0), ones], axis=1)
        for a0 in range(0, 4, DIL_GROUP):
            group = range(a0, a0 + DIL_GROUP)
            trows = slice(2 * a0 * BLK, 2 * (a0 + DIL_GROUP) * BLK)
            lhs = jnp.concatenate(
                [ref[0, p, a, 0] for a in group for ref in (qe_ref, qo_ref)], axis=0)
            s = jnp.minimum(_dot_nt(lhs, kcat), cap_ref[trows, :])
            m = jnp.max(s, axis=-1, keepdims=True)
            e = jnp.exp2(s - m)
            tile = lambda n, ap: e[2 * n * BLK:2 * (n + 1) * BLK, ap * BLK:(ap + 1) * BLK]
            pm = jnp.concatenate([
                jnp.concatenate([
                    tile(n, ap) * mult_ref[...] if ap == a else tile(n, ap)
                    for ap in range(4)], axis=1)
                for n, a in enumerate(group)], axis=0)
            acc = _dot(pm.astype(BF16), vext)
            den = acc[:, PAIR:]
            out = acc[:, :PAIR] / den
            lse = m + jnp.log(den) * LOG2_E
            for n, a in enumerate(group):
                rows = pl.ds(4 * a + r4, BLK, stride=DIL)
                o_ref[0, p, rows, :] = _pair_select(out[2 * n * BLK:2 * (n + 1) * BLK])
                lse_ref[0, p, rows, :] = _pair_select(lse[2 * n * BLK:2 * (n + 1) * BLK])


def _dil_tables():
    i = (np.arange(2 * BLK) % BLK)[:, None]
    j = np.arange(BLK)[None, :]
    cap = np.empty((4, 2 * BLK, 4, BLK), np.float32)
    for a in range(4):
        for ap in range(4):
            delta = 4 * (i - j) + (a - ap)
            in_d4 = (delta >= 0) & (delta <= BLK)
            in_d16 = (j <= i) & (a == ap)
            cap[a, :, ap, :] = np.where(in_d4 | in_d16, np.finfo(np.float32).max, NEG_INF)
    both = (i - j >= 0) & (4 * (i - j) <= BLK)
    mult = np.where(both, 2.0, 1.0).astype(np.float32)
    return jnp.asarray(cap.reshape(8 * BLK, 4 * BLK)), jnp.asarray(mult)


def _dil(qe, qo, k, v, cap, mult):
    b, _, _, nq, _ = k.shape
    t = nq * DIL
    by_res = lambda arr: arr.reshape(b, N_PAIRS, 4, 4, nq, PAIR)
    res_spec = pl.BlockSpec((1, N_PAIRS, 4, 1, nq, PAIR), lambda i, j: (i, 0, 0, j, 0, 0))
    nat_spec = pl.BlockSpec((1, N_PAIRS, t, PAIR), lambda i, j: (i, 0, 0, 0))
    nat_shape = jax.ShapeDtypeStruct((b, N_PAIRS, t, PAIR), F32)
    return pl.pallas_call(
        _dil_kernel,
        out_shape=[nat_shape, nat_shape],
        grid=(b, 4),
        in_specs=[res_spec, res_spec, res_spec, res_spec,
                  pl.BlockSpec(cap.shape, lambda i, j: (0, 0)),
                  pl.BlockSpec(mult.shape, lambda i, j: (0, 0))],
        out_specs=[nat_spec, nat_spec],
        compiler_params=_params(2),
    )(by_res(qe), by_res(qo), by_res(k), by_res(v), cap, mult)


def _hyb_main_kernel(x_ref, g_ref, qe_ref, qo_ref, k_ref, v_ref, o23_ref, l23_ref,
                     qre_ref, qro_ref, kr_ref, vr_ref, gr_ref,
                     cap_ref, decay_ref, zeta_ref, xi_ref, cd_ref, wout_ref,
                     o_ref, state_ref, m_scr):
    step = pl.program_id(1)

    @pl.when(step == 0)
    def _():
        state_ref[...] = jnp.zeros(state_ref.shape, F32)

    pairs = range(N_PAIRS)
    stack = lambda vals: jnp.concatenate(vals, axis=0)
    piece = lambda val, p, n: val[p * n:(p + 1) * n]
    lane = lax.broadcasted_iota(jnp.int32, (N_PAIRS * BLK, PAIR), 1)
    even = lane < HEAD_DIM
    row = lax.broadcasted_iota(jnp.int32, (N_PAIRS * BLK, PAIR), 0)
    same_head = lax.shift_right_logical(row & (PAIR - 1), 6) == lax.shift_right_logical(lane, 6)
    ones = jnp.ones((2 * BLK, PAIR), BF16)
    for c in range(MAIN_ROWS // BLK):
        rows = slice(c * BLK, (c + 1) * BLK)
        nb = step * (MAIN_ROWS // BLK) + c
        kstart = pl.multiple_of(jnp.maximum(nb - 1, 0) * BLK, BLK)
        cap = cap_ref[jnp.minimum(nb, 1)]
        s = stack([
            _dot_nt(jnp.concatenate([qe_ref[0, p, rows, :], qo_ref[0, p, rows, :]], axis=0),
                    k_ref[0, p, pl.ds(kstart, 2 * BLK), :]) for p in pairs])
        s = jnp.minimum(s, cap)
        m1 = jnp.max(s, axis=-1, keepdims=True)
        e = jnp.exp2(s - m1).astype(BF16)
        acc = [_dot(piece(e, p, 2 * BLK),
                    jnp.concatenate([v_ref[0, p, pl.ds(kstart, 2 * BLK), :], ones], axis=1))
               for p in pairs]
        num1 = stack([_pair_select(acc[p][:, :PAIR]) for p in pairs])
        den1 = stack([_pair_select(acc[p][:, PAIR:]) for p in pairs])
        m1 = stack([_pair_bcast(piece(m1, p, 2 * BLK)) for p in pairs])
        l23 = stack([l23_ref[0, p, rows, :] for p in pairs])
        o23 = stack([o23_ref[0, p, rows, :] for p in pairs])
        top = jnp.maximum(l23, m1)
        w23 = jnp.exp2(l23 - top)
        w1 = jnp.exp2(m1 - top)
        attn = ((o23 * w23 + num1 * w1) / (w23 + den1 * w1)).astype(BF16)
        qst = [jnp.concatenate([qre_ref[0, p, rows, :], qro_ref[0, p, rows, :]], axis=0)
               for p in pairs]
        kr = stack([kr_ref[0, p, rows, :] for p in pairs])
        vr = [vr_ref[0, p, rows, :] for p in pairs]
        kr16 = kr.astype(BF16)
        scores = stack([_dot_nt(qst[p], piece(kr16, p, BLK)) for p in pairs]) * decay_ref[...]
        scores = scores.astype(BF16)
        inner = stack([_pair_select(_dot(piece(scores, p, 2 * BLK), vr[p])) for p in pairs])
        state = state_ref[...]
        state16 = state.astype(BF16)
        cross2 = [_dot(qst[p], piece(state16, p, PAIR)) for p in pairs]
        cross = stack([c2[:BLK] + c2[BLK:] for c2 in cross2]) * xi_ref[...]
        kz = kr * zeta_ref[...]
        kv = stack([_dot(piece(kz, p, BLK).T.astype(BF16), vr[p]) for p in pairs])
        state_ref[...] = state * cd_ref[...] + jnp.where(same_head, kv, 0.0)
        out = inner + cross
        sq = out * out
        ms_e = jnp.sum(jnp.where(even, sq, 0.0), axis=-1, keepdims=True)
        ms_o = jnp.sum(jnp.where(even, 0.0, sq), axis=-1, keepdims=True)
        ms = jnp.where(even, ms_e, ms_o) * (1.0 / HEAD_DIM)
        out = out * lax.rsqrt(ms + EPS)
        gate = stack([gr_ref[0, p, rows, :] for p in pairs])
        ret = ((gate * jax.nn.sigmoid(gate)) * out).astype(BF16)
        for p in pairs:
            m_scr[rows, p * PAIR:(p + 1) * PAIR] = piece(attn, p, BLK)
            m_scr[rows, MIX_W + p * PAIR:MIX_W + (p + 1) * PAIR] = piece(ret, p, BLK)
    y = _dot(m_scr[...], wout_ref[...])
    o_ref[0] = x_ref[0] + _rms(y, g_ref[3:4, :])


def _window_cap():
    i = (np.arange(2 * BLK) % BLK)[:, None]
    j = np.arange(2 * BLK)[None, :]
    fmax = np.finfo(np.float32).max
    delta = np.stack([i - j, BLK + i - j])
    cap = np.where((delta >= 0) & (delta <= BLK), fmax, NEG_INF).astype(np.float32)
    return jnp.asarray(np.tile(cap, (1, N_PAIRS, 1)))


def _hyb_main(x, norm_g, qe, qo, k, v, o23, l23, qre, qro, kr, vr, gr, cap, consts, w_out,
              layer, j):
    b, t, d = x.shape
    n_g = norm_g.shape[1]
    blk = pl.BlockSpec((1, N_PAIRS, MAIN_ROWS, PAIR), lambda i, s: (i, 0, s, 0))
    full = pl.BlockSpec((1, N_PAIRS, t, PAIR), lambda i, s: (i, 0, 0, 0))
    xblk = pl.BlockSpec((1, MAIN_ROWS, d), lambda i, s: (i, s, 0))
    const3 = lambda arr: pl.BlockSpec(arr.shape, lambda i, s: (0,) * arr.ndim)
    decay, zeta, xi, cd = consts
    return pl.pallas_call(
        _hyb_main_kernel,
        out_shape=jax.ShapeDtypeStruct((b, t, d), F32),
        grid=(b, t // MAIN_ROWS),
        in_specs=[
            xblk,
            pl.BlockSpec((None, n_g, d), lambda i, s: (layer, 0, 0)),
            blk, blk, full, full, blk, blk,
            blk, blk, blk, blk, blk,
            const3(cap), const3(decay), const3(zeta), const3(xi), const3(cd),
            pl.BlockSpec((None, d, d), lambda i, s: (j, 0, 0)),
        ],
        out_specs=xblk,
        scratch_shapes=[pltpu.VMEM((N_PAIRS * PAIR, PAIR), F32),
                        pltpu.VMEM((MAIN_ROWS, d), BF16)],
        compiler_params=_params(2),
    )(x, norm_g, qe, qo, k, v, o23, l23, qre, qro, kr, vr, gr, cap, decay, zeta, xi, cd, w_out)


def _rope_table(t, rot_dims, theta):
    half = rot_dims // 2
    inv = theta ** (-(jnp.arange(half, dtype=F32) * 2.0 / rot_dims))
    ang = jnp.arange(t, dtype=F32)[:, None] * inv[None, :]
    cos, sin = jnp.cos(ang), jnp.sin(ang)
    rest = HEAD_DIM - rot_dims
    cos64 = jnp.concatenate([cos, cos, jnp.ones((t, rest), F32)], axis=-1)
    sin64 = jnp.concatenate([-sin, sin, jnp.zeros((t, rest), F32)], axis=-1)
    return jnp.tile(cos64, (1, 2)), jnp.tile(sin64, (1, 2))


def _retention_consts(n_heads):
    c = BLK
    log_g = jnp.log(1.0 - jnp.exp2(-5.0 - jnp.arange(n_heads, dtype=F32)))
    idx = jnp.arange(c, dtype=F32)
    diff = idx[:, None] - idx[None, :]
    decay = jnp.where(diff >= 0, jnp.exp(log_g[:, None, None] * jnp.maximum(diff, 0.0)), 0.0)
    zeta = jnp.exp(log_g[:, None] * (c - 1.0 - idx)[None, :])
    xi = jnp.exp(log_g[:, None] * (idx + 1.0)[None, :])
    chunk_decay = jnp.exp(log_g * c)
    per_lane = lambda hc: jnp.repeat(
        hc.reshape(N_PAIRS, 2, -1).transpose(0, 2, 1), HEAD_DIM, axis=-1).reshape(-1, PAIR)
    decay_st = decay.reshape(2 * N_PAIRS * c, c)
    chunk_decay = jnp.broadcast_to(chunk_decay[:, None], (n_heads, PAIR))
    return decay_st, per_lane(zeta), per_lane(xi), per_lane(chunk_decay)


def kernel(x, norm_g, ffn_w_gate, ffn_w_up, ffn_w_down, hyb_w_in, hyb_w_out,
           gmlp_w_in, gmlp_ln_g, gmlp_ln_b, gmlp_w_s, gmlp_b_s, gmlp_w_out):
    b, t, d = x.shape
    depth = norm_g.shape[0]
    wg, wu, wd = ffn_w_gate, ffn_w_up, ffn_w_down
    hyb_in, hyb_out = hyb_w_in.astype(BF16), hyb_w_out.astype(BF16)
    g_in, g_out = gmlp_w_in.astype(BF16), gmlp_w_out.astype(BF16)
    tabs = _rope_table(t, ROPE_DIMS, ROPE_THETA) + _rope_table(t, HEAD_DIM, RET_ROPE_THETA)
    consts = _retention_consts(2 * N_PAIRS)
    dil_cap, dil_mult = _dil_tables()
    win_cap = _window_cap()
    bias_full = jnp.repeat(jnp.swapaxes(gmlp_b_s, 1, 2), BLK, axis=2)
    ln_g, ln_b = gmlp_ln_g[:, None, :], gmlp_ln_b[:, None, :]
    x = x.reshape(b * t, d)
    for layer in range(depth):
        j = layer // 2
        x = _ffn(x, norm_g, wg, wu, wd, layer, 0)
        if layer % 2 == 0:
            (qae, qao, ka, va, q16e, q16o, k16, v16, qre, qro, kr, vr, gr) = _hyb_in(
                x, norm_g, hyb_in, tabs, layer, j, b, t)
            o23, l23 = _dil(q16e, q16o, k16, v16, dil_cap, dil_mult)
            x = _hyb_main(x.reshape(b, t, d), norm_g, qae, qao, ka, va, o23, l23,
                          qre, qro, kr, vr, gr, win_cap, consts, hyb_out, layer, j
                          ).reshape(b * t, d)
        else:
            x = _gmlp(x, norm_g, g_in, ln_g, ln_b, gmlp_w_s, bias_full, g_out, layer, j)
        x = _ffn(x, norm_g, wg, wu, wd, layer, 1)
    return x.reshape(b, t, d)
```

```python
import functools

import jax
import jax.numpy as jnp
import numpy as np
from jax import lax
from jax.experimental import pallas as pl
from jax.experimental.pallas import tpu as pltpu

F32 = jnp.float32
BF16 = jnp.bfloat16

D_MODEL = 1024
HEAD_DIM = 64
PAIR = 2 * HEAD_DIM
N_PAIRS = 4
MIX_W = N_PAIRS * PAIR
BLK = 128
DIL = 16
D_FF = 2816
FF_CHUNK = 256
GMLP_GROUPS = 8
ROPE_THETA = 500000.0
ROPE_DIMS = HEAD_DIM // 4
RET_ROPE_THETA = 10000.0
EPS = 1e-6
NEG_INF = -1e30
LOG2_E = np.float32(np.log2(np.e))
QK_SCALE = np.float32(HEAD_DIM ** -0.5 * np.log2(np.e))

FFN_ROWS = 512
FFN_NORM_GROUPS = 8
GMLP_ROWS = 512
IN_ROWS = 512
MAIN_ROWS = 512
DIL_GROUP = 2
VMEM_LIMIT = 60 * 1024 * 1024


def _dot(a, b):
    return jnp.dot(a, b, preferred_element_type=F32)


def _dot_nt(a, b):
    return lax.dot_general(a, b, (((1,), (1,)), ((), ())), preferred_element_type=F32)


def _rms(x, g):
    y = x * lax.rsqrt(jnp.mean(x * x, axis=-1, keepdims=True) + EPS)
    return y * g


def _gelu(x):
    return 0.5 * x * (1.0 + lax.erf(x * np.float32(np.sqrt(0.5))))


def _params(n_axes):
    return pltpu.CompilerParams(
        dimension_semantics=("arbitrary",) * n_axes, vmem_limit_bytes=VMEM_LIMIT)


def _zero_from(*vals):
    flags = None
    for v in vals:
        for r in range(0, v.shape[0], 8):
            for l in range(0, v.shape[1], PAIR):
                f = jnp.where(v[r:r + 8, l:l + PAIR] > 0.0, 1, 0)
                flags = f if flags is None else flags | f
    zero = lax.shift_right_logical(flags, 1).astype(F32)[0:1, :]
    return jnp.concatenate([zero] * (FF_CHUNK // PAIR), axis=1)


def _ffn_kernel(xprev_ref, xnext_ref, g_ref, wg_hbm, wu_hbm, wd_hbm, o_ref,
                h_scr, acc_scr, wg_ref, wu_ref, wd_ref, sem, *, g_row, n_tiles, layer, which):
    i = pl.program_id(0)
    g_in = g_ref[g_row:g_row + 1, :]
    g_out = g_ref[g_row + 1:g_row + 2, :]
    n_chunks = D_FF // FF_CHUNK

    def weight_copies(c):
        sl = slice(c * FF_CHUNK, (c + 1) * FF_CHUNK)
        return [
            pltpu.make_async_copy(wg_hbm.at[layer, which, :, sl], wg_ref.at[:, sl], sem.at[0, c]),
            pltpu.make_async_copy(wu_hbm.at[layer, which, :, sl], wu_ref.at[:, sl], sem.at[1, c]),
            pltpu.make_async_copy(wd_hbm.at[layer, which, sl, :], wd_ref.at[sl, :], sem.at[2, c]),
        ]

    def finish(rs):
        out_prev = xprev_ref[rs, :] + 0.5 * _rms(acc_scr[rs, :], g_out)
        o_ref[rs, :] = out_prev
        return out_prev

    def step(wait_weights):
        h = h_scr[...]
        zero_rows = []
        group = FFN_ROWS // FFN_NORM_GROUPS
        for j in range(FFN_NORM_GROUPS):
            rs = slice(j * group, (j + 1) * group)
            h_next = _rms(xnext_ref[rs, :], g_in)
            h_scr[rs, :] = h_next.astype(BF16)
            zero_rows.append(_zero_from(h_next, finish(rs)))
        acc = jnp.zeros(acc_scr.shape, F32)
        for c in range(n_chunks):
            sl = slice(c * FF_CHUNK, (c + 1) * FF_CHUNK)
            if wait_weights:
                for copy in weight_copies(c):
                    copy.wait()
            gate = _dot(h, wg_ref[:, sl].astype(BF16))
            up = _dot(h, wu_ref[:, sl].astype(BF16))
            if 1 <= c <= FFN_NORM_GROUPS:
                up = up + zero_rows[c - 1]
            act = (gate * jax.nn.sigmoid(gate)) * up
            acc = acc + _dot(act.astype(BF16), wd_ref[sl, :].astype(BF16))
        acc_scr[...] = acc

    @pl.when(i == 0)
    def _():
        for c in range(n_chunks):
            for copy in weight_copies(c):
                copy.start()
        h_scr[...] = _rms(xprev_ref[...], g_in).astype(BF16)
        acc_scr[...] = jnp.zeros(acc_scr.shape, F32)
        step(wait_weights=True)

    @pl.when(jnp.logical_and(i > 0, i < n_tiles))
    def _():
        step(wait_weights=False)

    @pl.when(i == n_tiles)
    def _():
        finish(slice(None))


def _ffn(x, norm_g, wg, wu, wd, layer, which):
    n, d = x.shape
    n_g = norm_g.shape[1]
    n_tiles = n // FFN_ROWS
    prev_tile = lambda i: (jnp.maximum(i - 1, 0), 0)
    next_tile = lambda i: (jnp.minimum(i + 1, n_tiles - 1), 0)
    in_hbm = pl.BlockSpec(memory_space=pl.ANY)
    return pl.pallas_call(
        functools.partial(_ffn_kernel, g_row=4 * which, n_tiles=n_tiles, layer=layer, which=which),
        out_shape=jax.ShapeDtypeStruct((n, d), F32),
        grid=(n_tiles + 1,),
        in_specs=[
            pl.BlockSpec((FFN_ROWS, d), prev_tile),
            pl.BlockSpec((FFN_ROWS, d), next_tile),
            pl.BlockSpec((None, n_g, d), lambda i: (layer, 0, 0)),
            in_hbm, in_hbm, in_hbm,
        ],
        out_specs=pl.BlockSpec((FFN_ROWS, d), prev_tile),
        scratch_shapes=[pltpu.VMEM((FFN_ROWS, d), BF16), pltpu.VMEM((FFN_ROWS, d), F32),
                        pltpu.VMEM((d, D_FF), F32), pltpu.VMEM((d, D_FF), F32),
                        pltpu.VMEM((D_FF, d), F32),
                        pltpu.SemaphoreType.DMA((3, D_FF // FF_CHUNK))],
        compiler_params=_params(1),
    )(x, x, norm_g, wg, wu, wd)


def _gmlp_kernel(x_ref, g_ref, win_ref, lng_ref, lnb_ref, ws_ref, bias_ref, wout_ref,
                 o_ref, m_scr):
    x = x_ref[...]
    rows, d = x.shape
    h = _rms(x, g_ref[2:3, :]).astype(BF16)
    chunk = lambda c0: _gelu(_dot(h, win_ref[:, c0:c0 + FF_CHUNK]))
    v = jnp.concatenate([chunk(d + c * FF_CHUNK) for c in range(d // FF_CHUNK)], axis=1)
    u = jnp.concatenate([chunk(c * FF_CHUNK) for c in range(d // FF_CHUNK)], axis=1)
    mu = jnp.mean(v, axis=-1, keepdims=True)
    var = jnp.mean(jnp.square(v - mu), axis=-1, keepdims=True)
    v = ((v - mu) * lax.rsqrt(var + EPS)) * lng_ref[...] + lnb_ref[...]
    ii = lax.broadcasted_iota(jnp.int32, (BLK, BLK), 0)
    jj = lax.broadcasted_iota(jnp.int32, (BLK, BLK), 1)
    causal = jj <= ii
    for grp in range(GMLP_GROUPS):
        lanes = slice(grp * BLK, (grp + 1) * BLK)
        w = jnp.where(causal, ws_ref[grp], 0.0).astype(BF16)
        for c in range(rows // BLK):
            rs = slice(c * BLK, (c + 1) * BLK)
            s = _dot(w, v[rs, lanes].astype(BF16)) + bias_ref[:, lanes]
            m_scr[rs, lanes] = (u[rs, lanes] * s).astype(BF16)
    y = _dot(m_scr[...], wout_ref[...])
    o_ref[...] = x + _rms(y, g_ref[3:4, :])


def _gmlp(x, norm_g, w_in, ln_g, ln_b, w_s, bias_full, w_out, layer, j):
    n, d = x.shape
    n_g = norm_g.shape[1]
    sel = lambda i: (j, 0, 0)
    return pl.pallas_call(
        _gmlp_kernel,
        out_shape=jax.ShapeDtypeStruct((n, d), F32),
        grid=(n // GMLP_ROWS,),
        in_specs=[
            pl.BlockSpec((GMLP_ROWS, d), lambda i: (i, 0)),
            pl.BlockSpec((None, n_g, d), lambda i: (layer, 0, 0)),
            pl.BlockSpec((None, d, 2 * d), sel),
            pl.BlockSpec((None, 1, d), sel),
            pl.BlockSpec((None, 1, d), sel),
            pl.BlockSpec((None, GMLP_GROUPS, BLK, BLK), lambda i: (j, 0, 0, 0)),
            pl.BlockSpec((None, BLK, d), sel),
            pl.BlockSpec((None, d, d), sel),
        ],
        out_specs=pl.BlockSpec((GMLP_ROWS, d), lambda i: (i, 0)),
        scratch_shapes=[pltpu.VMEM((GMLP_ROWS, d), BF16)],
        compiler_params=_params(1),
    )(x, norm_g, w_in, ln_g, ln_b, w_s, bias_full, w_out)


def _rope(z, cos, sin, half):
    lane = lax.broadcasted_iota(jnp.int32, z.shape, 1)
    up = pltpu.roll(z, PAIR - half, axis=1)
    dn = pltpu.roll(z, half, axis=1)
    partner = jnp.where((lane & (HEAD_DIM - 1)) < half, up, dn)
    return z * cos + partner * sin


def _hyb_in_kernel(x_ref, g_ref, w_ref, ca_ref, sa_ref, cr_ref, sr_ref,
                   qae_ref, qao_ref, ka_ref, va_ref,
                   q16e_ref, q16o_ref, k16_ref, v16_ref,
                   qre_ref, qro_ref, kr_ref, vr_ref, gr_ref, sort_scr, sort2_scr):
    h = _rms(x_ref[...], g_ref[2:3, :]).astype(BF16)
    ca, sa, cr, sr = ca_ref[...], sa_ref[...], cr_ref[...], sr_ref[...]
    even = lax.broadcasted_iota(jnp.int32, (IN_ROWS, PAIR), 1) < HEAD_DIM
    even_s = lax.broadcasted_iota(jnp.int32, (IN_ROWS // DIL, PAIR), 1) < HEAD_DIM

    def proj(group, half):
        c0 = group * MIX_W + half * 2 * PAIR
        z = _dot(h, w_ref[:, c0:c0 + 2 * PAIR])
        return [(2 * half, z[:, :PAIR]), (2 * half + 1, z[:, PAIR:])]

    def sorted_rows(val):
        sort_scr[...] = val
        quarter = IN_ROWS // 4
        for r4 in range(4):
            sort2_scr[r4 * quarter:(r4 + 1) * quarter, :] = sort_scr[pl.ds(r4, quarter, stride=4), :]
        return [sort2_scr[pl.ds((r % 4) * quarter + r // 4, IN_ROWS // DIL, stride=4), :]
                for r in range(DIL)]

    for half in range(2):
        for p, z in proj(0, half):
            qa = _rope(z, ca, sa, ROPE_DIMS // 2) * QK_SCALE
            qae_ref[0, p] = jnp.where(even, qa, 0.0).astype(BF16)
            qao_ref[0, p] = jnp.where(even, 0.0, qa).astype(BF16)
            for r, rows in enumerate(sorted_rows(qa)):
                q16e_ref[0, p, r] = jnp.where(even_s, rows, 0.0).astype(BF16)
                q16o_ref[0, p, r] = jnp.where(even_s, 0.0, rows).astype(BF16)
        for p, z in proj(3, half):
            qr = _rope(z, cr, sr, HEAD_DIM // 2)
            qre_ref[0, p] = jnp.where(even, qr, 0.0).astype(BF16)
            qro_ref[0, p] = jnp.where(even, 0.0, qr).astype(BF16)
        for p, z in proj(1, half):
            ka = _rope(z, ca, sa, ROPE_DIMS // 2)
            ka_ref[0, p] = ka.astype(BF16)
            for r, rows in enumerate(sorted_rows(ka)):
                k16_ref[0, p, r] = rows.astype(BF16)
        for p, z in proj(4, half):
            kr_ref[0, p] = _rope(z, cr, sr, HEAD_DIM // 2) * (HEAD_DIM ** -0.5)
        for p, z in proj(2, half):
            va_ref[0, p] = z.astype(BF16)
            for r, rows in enumerate(sorted_rows(z)):
                v16_ref[0, p, r] = rows.astype(BF16)
        for p, z in proj(5, half):
            vr_ref[0, p] = z.astype(BF16)
        for p, z in proj(6, half):
            gr_ref[0, p] = z


def _hyb_in(x, norm_g, w_in, tabs, layer, j, b, t):
    n, d = x.shape
    n_g = norm_g.shape[1]
    per_b = t // IN_ROWS
    nat_spec = pl.BlockSpec((1, N_PAIRS, IN_ROWS, PAIR), lambda i: (i // per_b, 0, i % per_b, 0))
    sorted_spec = pl.BlockSpec((1, N_PAIRS, DIL, IN_ROWS // DIL, PAIR),
                               lambda i: (i // per_b, 0, 0, i % per_b, 0))
    tab_spec = pl.BlockSpec((IN_ROWS, PAIR), lambda i: (i % per_b, 0))
    nat = lambda dt: jax.ShapeDtypeStruct((b, N_PAIRS, t, PAIR), dt)
    srt = jax.ShapeDtypeStruct((b, N_PAIRS, DIL, t // DIL, PAIR), BF16)
    return pl.pallas_call(
        _hyb_in_kernel,
        out_shape=[nat(BF16)] * 4 + [srt] * 4 + [nat(BF16), nat(BF16), nat(F32), nat(BF16), nat(F32)],
        grid=(n // IN_ROWS,),
        in_specs=[
            pl.BlockSpec((IN_ROWS, d), lambda i: (i, 0)),
            pl.BlockSpec((None, n_g, d), lambda i: (layer, 0, 0)),
            pl.BlockSpec((None,) + w_in.shape[1:], lambda i: (j, 0, 0)),
            tab_spec, tab_spec, tab_spec, tab_spec,
        ],
        out_specs=[nat_spec] * 4 + [sorted_spec] * 4 + [nat_spec] * 5,
        scratch_shapes=[pltpu.VMEM((IN_ROWS, PAIR), F32), pltpu.VMEM((IN_ROWS, PAIR), F32)],
        compiler_params=_params(1),
    )(x, norm_g, w_in, *tabs)


def _pair_select(top_bottom):
    lane = lax.broadcasted_iota(jnp.int32, (BLK, PAIR), 1)
    return jnp.where(lane < HEAD_DIM, top_bottom[:BLK], top_bottom[BLK:])


def _pair_bcast(col):
    lane = lax.broadcasted_iota(jnp.int32, (BLK, PAIR), 1)
    return jnp.where(lane < HEAD_DIM, col[:BLK], col[BLK:])


def _dil_kernel(qe_ref, qo_ref, k_ref, v_ref, cap_ref, mult_ref, o_ref, lse_ref):
    r4 = pl.program_id(1)
    ones = jnp.ones((4 * BLK, PAIR), BF16)
    for p in range(N_PAIRS):
        kcat = jnp.concatenate([k_ref[0, p, ap, 0] for ap in range(4)], axis=0)
        vext = jnp.concatenate(
            [jnp.concatenate([v_ref[0, p, ap, 0] for ap in range(4)], axis=0), ones], axis=1)
        for a0 in range(0, 4, DIL_GROUP):
            group = range(a0, a0 + DIL_GROUP)
            trows = slice(2 * a0 * BLK, 2 * (a0 + DIL_GROUP) * BLK)
            lhs = jnp.concatenate(
                [ref[0, p, a, 0] for a in group for ref in (qe_ref, qo_ref)], axis=0)
            s = jnp.minimum(_dot_nt(lhs, kcat), cap_ref[trows, :])
            m = jnp.max(s, axis=-1, keepdims=True)
            e = jnp.exp2(s - m)
            tile = lambda n, ap: e[2 * n * BLK:2 * (n + 1) * BLK, ap * BLK:(ap + 1) * BLK]
            pm = jnp.concatenate([
                jnp.concatenate([
                    tile(n, ap) * mult_ref[...] if ap == a else tile(n, ap)
                    for ap in range(4)], axis=1)
                for n, a in enumerate(group)], axis=0)
            acc = _dot(pm.astype(BF16), vext)
            den = acc[:, PAIR:]
            out = acc[:, :PAIR] / den
            lse = m + jnp.log(den) * LOG2_E
            for n, a in enumerate(group):
                rows = pl.ds(4 * a + r4, BLK, stride=DIL)
                o_ref[0, p, rows, :] = _pair_select(out[2 * n * BLK:2 * (n + 1) * BLK])
                lse_ref[0, p, rows, :] = _pair_select(lse[2 * n * BLK:2 * (n + 1) * BLK])


def _dil_tables():
    i = (np.arange(2 * BLK) % BLK)[:, None]
    j = np.arange(BLK)[None, :]
    cap = np.empty((4, 2 * BLK, 4, BLK), np.float32)
    for a in range(4):
        for ap in range(4):
            delta = 4 * (i - j) + (a - ap)
            in_d4 = (delta >= 0) & (delta <= BLK)
            in_d16 = (j <= i) & (a == ap)
            cap[a, :, ap, :] = np.where(in_d4 | in_d16, np.finfo(np.float32).max, NEG_INF)
    both = (i - j >= 0) & (4 * (i - j) <= BLK)
    mult = np.where(both, 2.0, 1.0).astype(np.float32)
    return jnp.asarray(cap.reshape(8 * BLK, 4 * BLK)), jnp.asarray(mult)


def _dil(qe, qo, k, v, cap, mult):
    b, _, _, nq, _ = k.shape
    t = nq * DIL
    by_res = lambda arr: arr.reshape(b, N_PAIRS, 4, 4, nq, PAIR)
    res_spec = pl.BlockSpec((1, N_PAIRS, 4, 1, nq, PAIR), lambda i, j: (i, 0, 0, j, 0, 0))
    nat_spec = pl.BlockSpec((1, N_PAIRS, t, PAIR), lambda i, j: (i, 0, 0, 0))
    nat_shape = jax.ShapeDtypeStruct((b, N_PAIRS, t, PAIR), F32)
    return pl.pallas_call(
        _dil_kernel,
        out_shape=[nat_shape, nat_shape],
        grid=(b, 4),
        in_specs=[res_spec, res_spec, res_spec, res_spec,
                  pl.BlockSpec(cap.shape, lambda i, j: (0, 0)),
                  pl.BlockSpec(mult.shape, lambda i, j: (0, 0))],
        out_specs=[nat_spec, nat_spec],
        compiler_params=_params(2),
    )(by_res(qe), by_res(qo), by_res(k), by_res(v), cap, mult)


def _hyb_main_kernel(x_ref, g_ref, qe_ref, qo_ref, k_ref, v_ref, o23_ref, l23_ref,
                     qre_ref, qro_ref, kr_ref, vr_ref, gr_ref,
                     cap_ref, decay_ref, zeta_ref, xi_ref, cd_ref, wout_ref,
                     o_ref, state_ref, m_scr):
    step = pl.program_id(1)

    @pl.when(step == 0)
    def _():
        state_ref[...] = jnp.zeros(state_ref.shape, F32)

    pairs = range(N_PAIRS)
    stack = lambda vals: jnp.concatenate(vals, axis=0)
    piece = lambda val, p, n: val[p * n:(p + 1) * n]
    lane = lax.broadcasted_iota(jnp.int32, (N_PAIRS * BLK, PAIR), 1)
    even = lane < HEAD_DIM
    row = lax.broadcasted_iota(jnp.int32, (N_PAIRS * BLK, PAIR), 0)
    same_head = lax.shift_right_logical(row & (PAIR - 1), 6) == lax.shift_right_logical(lane, 6)
    ones = jnp.ones((2 * BLK, PAIR), BF16)
    for c in range(MAIN_ROWS // BLK):
        rows = slice(c * BLK, (c + 1) * BLK)
        nb = step * (MAIN_ROWS // BLK) + c
        kstart = pl.multiple_of(jnp.maximum(nb - 1, 0) * BLK, BLK)
        cap = cap_ref[jnp.minimum(nb, 1)]
        s = stack([
            _dot_nt(jnp.concatenate([qe_ref[0, p, rows, :], qo_ref[0, p, rows, :]], axis=0),
                    k_ref[0, p, pl.ds(kstart, 2 * BLK), :]) for p in pairs])
        s = jnp.minimum(s, cap)
        m1 = jnp.max(s, axis=-1, keepdims=True)
        e = jnp.exp2(s - m1).astype(BF16)
        acc = [_dot(piece(e, p, 2 * BLK),
                    jnp.concatenate([v_ref[0, p, pl.ds(kstart, 2 * BLK), :], ones], axis=1))
               for p in pairs]
        num1 = stack([_pair_select(acc[p][:, :PAIR]) for p in pairs])
        den1 = stack([_pair_select(acc[p][:, PAIR:]) for p in pairs])
        m1 = stack([_pair_bcast(piece(m1, p, 2 * BLK)) for p in pairs])
        l23 = stack([l23_ref[0, p, rows, :] for p in pairs])
        o23 = stack([o23_ref[0, p, rows, :] for p in pairs])
        top = jnp.maximum(l23, m1)
        w23 = jnp.exp2(l23 - top)
        w1 = jnp.exp2(m1 - top)
        attn = ((o23 * w23 + num1 * w1) / (w23 + den1 * w1)).astype(BF16)
        qst = [jnp.concatenate([qre_ref[0, p, rows, :], qro_ref[0, p, rows, :]], axis=0)
               for p in pairs]
        kr = stack([kr_ref[0, p, rows, :] for p in pairs])
        vr = [vr_ref[0, p, rows, :] for p in pairs]
        kr16 = kr.astype(BF16)
        scores = stack([_dot_nt(qst[p], piece(kr16, p, BLK)) for p in pairs]) * decay_ref[...]
        scores = scores.astype(BF16)
        inner = stack([_pair_select(_dot(piece(scores, p, 2 * BLK), vr[p])) for p in pairs])
        state = state_ref[...]
        state16 = state.astype(BF16)
        cross2 = [_dot(qst[p], piece(state16, p, PAIR)) for p in pairs]
        cross = stack([c2[:BLK] + c2[BLK:] for c2 in cross2]) * xi_ref[...]
        kz = kr * zeta_ref[...]
        kv = stack([_dot(piece(kz, p, BLK).T.astype(BF16), vr[p]) for p in pairs])
        state_ref[...] = state * cd_ref[...] + jnp.where(same_head, kv, 0.0)
        out = inner + cross
        sq = out * out
        ms_e = jnp.sum(jnp.where(even, sq, 0.0), axis=-1, keepdims=True)
        ms_o = jnp.sum(jnp.where(even, 0.0, sq), axis=-1, keepdims=True)
        ms = jnp.where(even, ms_e, ms_o) * (1.0 / HEAD_DIM)
        out = out * lax.rsqrt(ms + EPS)
        gate = stack([gr_ref[0, p, rows, :] for p in pairs])
        ret = ((gate * jax.nn.sigmoid(gate)) * out).astype(BF16)
        for p in pairs:
            m_scr[rows, p * PAIR:(p + 1) * PAIR] = piece(attn, p, BLK)
            m_scr[rows, MIX_W + p * PAIR:MIX_W + (p + 1) * PAIR] = piece(ret, p, BLK)
    y = _dot(m_scr[...], wout_ref[...])
    o_ref[0] = x_ref[0] + _rms(y, g_ref[3:4, :])


def _window_cap():
    i = (np.arange(2 * BLK) % BLK)[:, None]
    j = np.arange(2 * BLK)[None, :]
    fmax = np.finfo(np.float32).max
    delta = np.stack([i - j, BLK + i - j])
    cap = np.where((delta >= 0) & (delta <= BLK), fmax, NEG_INF).astype(np.float32)
    return jnp.asarray(np.tile(cap, (1, N_PAIRS, 1)))


def _hyb_main(x, norm_g, qe, qo, k, v, o23, l23, qre, qro, kr, vr, gr, cap, consts, w_out,
              layer, j):
    b, t, d = x.shape
    n_g = norm_g.shape[1]
    blk = pl.BlockSpec((1, N_PAIRS, MAIN_ROWS, PAIR), lambda i, s: (i, 0, s, 0))
    full = pl.BlockSpec((1, N_PAIRS, t, PAIR), lambda i, s: (i, 0, 0, 0))
    xblk = pl.BlockSpec((1, MAIN_ROWS, d), lambda i, s: (i, s, 0))
    const3 = lambda arr: pl.BlockSpec(arr.shape, lambda i, s: (0,) * arr.ndim)
    decay, zeta, xi, cd = consts
    return pl.pallas_call(
        _hyb_main_kernel,
        out_shape=jax.ShapeDtypeStruct((b, t, d), F32),
        grid=(b, t // MAIN_ROWS),
        in_specs=[
            xblk,
            pl.BlockSpec((None, n_g, d), lambda i, s: (layer, 0, 0)),
            blk, blk, full, full, blk, blk,
            blk, blk, blk, blk, blk,
            const3(cap), const3(decay), const3(zeta), const3(xi), const3(cd),
            pl.BlockSpec((None, d, d), lambda i, s: (j, 0, 0)),
        ],
        out_specs=xblk,
        scratch_shapes=[pltpu.VMEM((N_PAIRS * PAIR, PAIR), F32),
                        pltpu.VMEM((MAIN_ROWS, d), BF16)],
        compiler_params=_params(2),
    )(x, norm_g, qe, qo, k, v, o23, l23, qre, qro, kr, vr, gr, cap, decay, zeta, xi, cd, w_out)


def _rope_table(t, rot_dims, theta):
    half = rot_dims // 2
    inv = theta ** (-(jnp.arange(half, dtype=F32) * 2.0 / rot_dims))
    ang = jnp.arange(t, dtype=F32)[:, None] * inv[None, :]
    cos, sin = jnp.cos(ang), jnp.sin(ang)
    rest = HEAD_DIM - rot_dims
    cos64 = jnp.concatenate([cos, cos, jnp.ones((t, rest), F32)], axis=-1)
    sin64 = jnp.concatenate([-sin, sin, jnp.zeros((t, rest), F32)], axis=-1)
    return jnp.tile(cos64, (1, 2)), jnp.tile(sin64, (1, 2))


def _retention_consts(n_heads):
    c = BLK
    log_g = jnp.log(1.0 - jnp.exp2(-5.0 - jnp.arange(n_heads, dtype=F32)))
    idx = jnp.arange(c, dtype=F32)
    diff = idx[:, None] - idx[None, :]
    decay = jnp.where(diff >= 0, jnp.exp(log_g[:, None, None] * jnp.maximum(diff, 0.0)), 0.0)
    zeta = jnp.exp(log_g[:, None] * (c - 1.0 - idx)[None, :])
    xi = jnp.exp(log_g[:, None] * (idx + 1.0)[None, :])
    chunk_decay = jnp.exp(log_g * c)
    per_lane = lambda hc: jnp.repeat(
        hc.reshape(N_PAIRS, 2, -1).transpose(0, 2, 1), HEAD_DIM, axis=-1).reshape(-1, PAIR)
    decay_st = decay.reshape(2 * N_PAIRS * c, c)
    chunk_decay = jnp.broadcast_to(chunk_decay[:, None], (n_heads, PAIR))
    return decay_st, per_lane(zeta), per_lane(xi), per_lane(chunk_decay)


def kernel(x, norm_g, ffn_w_gate, ffn_w_up, ffn_w_down, hyb_w_in, hyb_w_out,
           gmlp_w_in, gmlp_ln_g, gmlp_ln_b, gmlp_w_s, gmlp_b_s, gmlp_w_out):
    b, t, d = x.shape
    depth = norm_g.shape[0]
    wg, wu, wd = ffn_w_gate, ffn_w_up, ffn_w_down
    hyb_in, hyb_out = hyb_w_in.astype(BF16), hyb_w_out.astype(BF16)
    g_in, g_out = gmlp_w_in.astype(BF16), gmlp_w_out.astype(BF16)
    tabs = _rope_table(t, ROPE_DIMS, ROPE_THETA) + _rope_table(t, HEAD_DIM, RET_ROPE_THETA)
    consts = _retention_consts(2 * N_PAIRS)
    dil_cap, dil_mult = _dil_tables()
    win_cap = _window_cap()
    bias_full = jnp.repeat(jnp.swapaxes(gmlp_b_s, 1, 2), BLK, axis=2)
    ln_g, ln_b = gmlp_ln_g[:, None, :], gmlp_ln_b[:, None, :]
    x = x.reshape(b * t, d)
    for layer in range(depth):
        j = layer // 2
        x = _ffn(x, norm_g, wg, wu, wd, layer, 0)
        if layer % 2 == 0:
            (qae, qao, ka, va, q16e, q16o, k16, v16, qre, qro, kr, vr, gr) = _hyb_in(
                x, norm_g, hyb_in, tabs, layer, j, b, t)
            o23, l23 = _dil(q16e, q16o, k16, v16, dil_cap, dil_mult)
            x = _hyb_main(x.reshape(b, t, d), norm_g, qae, qao, ka, va, o23, l23,
                          qre, qro, kr, vr, gr, win_cap, consts, hyb_out, layer, j
                          ).reshape(b * t, d)
        else:
            x = _gmlp(x, norm_g, g_in, ln_g, ln_b, gmlp_w_s, bias_full, g_out, layer, j)
        x = _ffn(x, norm_g, wg, wu, wd, layer, 1)
    return x.reshape(b, t, d)
```

```python
import functools

import jax
import jax.numpy as jnp
import numpy as np
from jax import lax
from jax.experimental import pallas as pl
from jax.experimental.pallas import tpu as pltpu

F32 = jnp.float32
BF16 = jnp.bfloat16

D_MODEL = 1024
HEAD_DIM = 64
PAIR = 2 * HEAD_DIM
N_PAIRS = 4
MIX_W = N_PAIRS * PAIR
BLK = 128
DIL = 16
D_FF = 2816
FF_CHUNK = 256
GMLP_GROUPS = 8
ROPE_THETA = 500000.0
ROPE_DIMS = HEAD_DIM // 4
RET_ROPE_THETA = 10000.0
EPS = 1e-6
NEG_INF = -1e30
LOG2_E = np.float32(np.log2(np.e))
QK_SCALE = np.float32(HEAD_DIM ** -0.5 * np.log2(np.e))

FFN_ROWS = 512
FFN_NORM_GROUPS = 8
GMLP_ROWS = 512
IN_ROWS = 1024
MAIN_ROWS = 512
DIL_GROUP = 2
VMEM_LIMIT = 60 * 1024 * 1024


def _dot(a, b):
    return jnp.dot(a, b, preferred_element_type=F32)


def _dot_nt(a, b):
    return lax.dot_general(a, b, (((1,), (1,)), ((), ())), preferred_element_type=F32)


def _rms(x, g):
    y = x * lax.rsqrt(jnp.mean(x * x, axis=-1, keepdims=True) + EPS)
    return y * g


def _gelu(x):
    return 0.5 * x * (1.0 + lax.erf(x * np.float32(np.sqrt(0.5))))


def _params(n_axes):
    return pltpu.CompilerParams(
        dimension_semantics=("arbitrary",) * n_axes, vmem_limit_bytes=VMEM_LIMIT)


def _zero_from(*vals):
    flags = None
    for v in vals:
        for r in range(0, v.shape[0], 8):
            for l in range(0, v.shape[1], PAIR):
                f = jnp.where(v[r:r + 8, l:l + PAIR] > 0.0, 1, 0)
                flags = f if flags is None else flags | f
    zero = lax.shift_right_logical(flags, 1).astype(F32)[0:1, :]
    return jnp.concatenate([zero] * (FF_CHUNK // PAIR), axis=1)


def _ffn_kernel(xprev_ref, xnext_ref, g_ref, wg_hbm, wu_hbm, wd_hbm, o_ref,
                h_scr, acc_scr, wg_ref, wu_ref, wd_ref, sem, *, g_row, n_tiles, layer, which):
    i = pl.program_id(0)
    g_in = g_ref[g_row:g_row + 1, :]
    g_out = g_ref[g_row + 1:g_row + 2, :]
    n_chunks = D_FF // FF_CHUNK

    def weight_copies(c):
        sl = slice(c * FF_CHUNK, (c + 1) * FF_CHUNK)
        return [
            pltpu.make_async_copy(wg_hbm.at[layer, which, :, sl], wg_ref.at[:, sl], sem.at[0, c]),
            pltpu.make_async_copy(wu_hbm.at[layer, which, :, sl], wu_ref.at[:, sl], sem.at[1, c]),
            pltpu.make_async_copy(wd_hbm.at[layer, which, sl, :], wd_ref.at[sl, :], sem.at[2, c]),
        ]

    def finish(rs):
        out_prev = xprev_ref[rs, :] + 0.5 * _rms(acc_scr[rs, :], g_out)
        o_ref[rs, :] = out_prev
        return out_prev

    def step(wait_weights):
        h = h_scr[...]
        zero_rows = []
        group = FFN_ROWS // FFN_NORM_GROUPS
        for j in range(FFN_NORM_GROUPS):
            rs = slice(j * group, (j + 1) * group)
            h_next = _rms(xnext_ref[rs, :], g_in)
            h_scr[rs, :] = h_next.astype(BF16)
            zero_rows.append(_zero_from(h_next, finish(rs)))
        acc = jnp.zeros(acc_scr.shape, F32)
        for c in range(n_chunks):
            sl = slice(c * FF_CHUNK, (c + 1) * FF_CHUNK)
            if wait_weights:
                for copy in weight_copies(c):
                    copy.wait()
            gate = _dot(h, wg_ref[:, sl].astype(BF16))
            up = _dot(h, wu_ref[:, sl].astype(BF16))
            if 1 <= c <= FFN_NORM_GROUPS:
                up = up + zero_rows[c - 1]
            act = (gate * jax.nn.sigmoid(gate)) * up
            acc = acc + _dot(act.astype(BF16), wd_ref[sl, :].astype(BF16))
        acc_scr[...] = acc

    @pl.when(i == 0)
    def _():
        for c in range(n_chunks):
            for copy in weight_copies(c):
                copy.start()
        h_scr[...] = _rms(xprev_ref[...], g_in).astype(BF16)
        acc_scr[...] = jnp.zeros(acc_scr.shape, F32)
        step(wait_weights=True)

    @pl.when(jnp.logical_and(i > 0, i < n_tiles))
    def _():
        step(wait_weights=False)

    @pl.when(i == n_tiles)
    def _():
        finish(slice(None))


def _ffn(x, norm_g, wg, wu, wd, layer, which):
    n, d = x.shape
    n_g = norm_g.shape[1]
    n_tiles = n // FFN_ROWS
    prev_tile = lambda i: (jnp.maximum(i - 1, 0), 0)
    next_tile = lambda i: (jnp.minimum(i + 1, n_tiles - 1), 0)
    in_hbm = pl.BlockSpec(memory_space=pl.ANY)
    return pl.pallas_call(
        functools.partial(_ffn_kernel, g_row=4 * which, n_tiles=n_tiles, layer=layer, which=which),
        out_shape=jax.ShapeDtypeStruct((n, d), F32),
        grid=(n_tiles + 1,),
        in_specs=[
            pl.BlockSpec((FFN_ROWS, d), prev_tile),
            pl.BlockSpec((FFN_ROWS, d), next_tile),
            pl.BlockSpec((None, n_g, d), lambda i: (layer, 0, 0)),
            in_hbm, in_hbm, in_hbm,
        ],
        out_specs=pl.BlockSpec((FFN_ROWS, d), prev_tile),
        scratch_shapes=[pltpu.VMEM((FFN_ROWS, d), BF16), pltpu.VMEM((FFN_ROWS, d), F32),
                        pltpu.VMEM((d, D_FF), F32), pltpu.VMEM((d, D_FF), F32),
                        pltpu.VMEM((D_FF, d), F32),
                        pltpu.SemaphoreType.DMA((3, D_FF // FF_CHUNK))],
        compiler_params=_params(1),
    )(x, x, norm_g, wg, wu, wd)


def _gmlp_kernel(x_ref, g_ref, win_ref, lng_ref, lnb_ref, ws_ref, bias_ref, wout_ref,
                 o_ref, m_scr):
    x = x_ref[...]
    rows, d = x.shape
    h = _rms(x, g_ref[2:3, :]).astype(BF16)
    chunk = lambda c0: _gelu(_dot(h, win_ref[:, c0:c0 + FF_CHUNK]))
    v = jnp.concatenate([chunk(d + c * FF_CHUNK) for c in range(d // FF_CHUNK)], axis=1)
    u = jnp.concatenate([chunk(c * FF_CHUNK) for c in range(d // FF_CHUNK)], axis=1)
    mu = jnp.mean(v, axis=-1, keepdims=True)
    var = jnp.mean(jnp.square(v - mu), axis=-1, keepdims=True)
    v = ((v - mu) * lax.rsqrt(var + EPS)) * lng_ref[...] + lnb_ref[...]
    ii = lax.broadcasted_iota(jnp.int32, (BLK, BLK), 0)
    jj = lax.broadcasted_iota(jnp.int32, (BLK, BLK), 1)
    causal = jj <= ii
    for grp in range(GMLP_GROUPS):
        lanes = slice(grp * BLK, (grp + 1) * BLK)
        w = jnp.where(causal, ws_ref[grp], 0.0).astype(BF16)
        for c in range(rows // BLK):
            rs = slice(c * BLK, (c + 1) * BLK)
            s = _dot(w, v[rs, lanes].astype(BF16)) + bias_ref[:, lanes]
            m_scr[rs, lanes] = (u[rs, lanes] * s).astype(BF16)
    y = _dot(m_scr[...], wout_ref[...])
    o_ref[...] = x + _rms(y, g_ref[3:4, :])


def _gmlp(x, norm_g, w_in, ln_g, ln_b, w_s, bias_full, w_out, layer, j):
    n, d = x.shape
    n_g = norm_g.shape[1]
    sel = lambda i: (j, 0, 0)
    return pl.pallas_call(
        _gmlp_kernel,
        out_shape=jax.ShapeDtypeStruct((n, d), F32),
        grid=(n // GMLP_ROWS,),
        in_specs=[
            pl.BlockSpec((GMLP_ROWS, d), lambda i: (i, 0)),
            pl.BlockSpec((None, n_g, d), lambda i: (layer, 0, 0)),
            pl.BlockSpec((None, d, 2 * d), sel),
            pl.BlockSpec((None, 1, d), sel),
            pl.BlockSpec((None, 1, d), sel),
            pl.BlockSpec((None, GMLP_GROUPS, BLK, BLK), lambda i: (j, 0, 0, 0)),
            pl.BlockSpec((None, BLK, d), sel),
            pl.BlockSpec((None, d, d), sel),
        ],
        out_specs=pl.BlockSpec((GMLP_ROWS, d), lambda i: (i, 0)),
        scratch_shapes=[pltpu.VMEM((GMLP_ROWS, d), BF16)],
        compiler_params=_params(1),
    )(x, norm_g, w_in, ln_g, ln_b, w_s, bias_full, w_out)


def _rope(z, cos, sin, half):
    lane = lax.broadcasted_iota(jnp.int32, z.shape, 1)
    up = pltpu.roll(z, PAIR - half, axis=1)
    dn = pltpu.roll(z, half, axis=1)
    partner = jnp.where((lane & (HEAD_DIM - 1)) < half, up, dn)
    return z * cos + partner * sin


def _hyb_in_kernel(x_ref, g_ref, w_ref, ca_ref, sa_ref, cr_ref, sr_ref,
                   qa_ref, ka_ref, va_ref, q16_ref, k16_ref, v16_ref,
                   qr_ref, kr_ref, vr_ref, gr_ref, sort_scr, sort2_scr):
    h = _rms(x_ref[...], g_ref[2:3, :]).astype(BF16)
    ca, sa, cr, sr = ca_ref[...], sa_ref[...], cr_ref[...], sr_ref[...]

    def proj(group, half):
        c0 = group * MIX_W + half * 2 * PAIR
        z = _dot(h, w_ref[:, c0:c0 + 2 * PAIR])
        return [(2 * half, z[:, :PAIR]), (2 * half + 1, z[:, PAIR:])]

    def sorted_rows(val):
        sort_scr[...] = val
        quarter = IN_ROWS // 4
        for r4 in range(4):
            sort2_scr[r4 * quarter:(r4 + 1) * quarter, :] = sort_scr[pl.ds(r4, quarter, stride=4), :]
        return [sort2_scr[pl.ds((r % 4) * quarter + r // 4, IN_ROWS // DIL, stride=4), :]
                for r in range(DIL)]

    for half in range(2):
        for p, z in proj(0, half):
            qa = _rope(z, ca, sa, ROPE_DIMS // 2) * QK_SCALE
            qa_ref[0, p] = qa.astype(BF16)
            for r, rows in enumerate(sorted_rows(qa)):
                q16_ref[0, p, r] = rows.astype(BF16)
        for p, z in proj(3, half):
            qr_ref[0, p] = _rope(z, cr, sr, HEAD_DIM // 2).astype(BF16)
        for p, z in proj(1, half):
            ka = _rope(z, ca, sa, ROPE_DIMS // 2)
            ka_ref[0, p] = ka.astype(BF16)
            for r, rows in enumerate(sorted_rows(ka)):
                k16_ref[0, p, r] = rows.astype(BF16)
        for p, z in proj(4, half):
            kr_ref[0, p] = _rope(z, cr, sr, HEAD_DIM // 2) * (HEAD_DIM ** -0.5)
        for p, z in proj(2, half):
            va_ref[0, p] = z.astype(BF16)
            for r, rows in enumerate(sorted_rows(z)):
                v16_ref[0, p, r] = rows.astype(BF16)
        for p, z in proj(5, half):
            vr_ref[0, p] = z.astype(BF16)
        for p, z in proj(6, half):
            gr_ref[0, p] = z


def _hyb_in(x, norm_g, w_in, tabs, layer, j, b, t):
    n, d = x.shape
    n_g = norm_g.shape[1]
    per_b = t // IN_ROWS
    nat_spec = pl.BlockSpec((1, N_PAIRS, IN_ROWS, PAIR), lambda i: (i // per_b, 0, i % per_b, 0))
    sorted_spec = pl.BlockSpec((1, N_PAIRS, DIL, IN_ROWS // DIL, PAIR),
                               lambda i: (i // per_b, 0, 0, i % per_b, 0))
    tab_spec = pl.BlockSpec((IN_ROWS, PAIR), lambda i: (i % per_b, 0))
    nat = lambda dt: jax.ShapeDtypeStruct((b, N_PAIRS, t, PAIR), dt)
    srt = jax.ShapeDtypeStruct((b, N_PAIRS, DIL, t // DIL, PAIR), BF16)
    return pl.pallas_call(
        _hyb_in_kernel,
        out_shape=[nat(BF16)] * 3 + [srt] * 3 + [nat(BF16), nat(F32), nat(BF16), nat(F32)],
        grid=(n // IN_ROWS,),
        in_specs=[
            pl.BlockSpec((IN_ROWS, d), lambda i: (i, 0)),
            pl.BlockSpec((None, n_g, d), lambda i: (layer, 0, 0)),
            pl.BlockSpec((None,) + w_in.shape[1:], lambda i: (j, 0, 0)),
            tab_spec, tab_spec, tab_spec, tab_spec,
        ],
        out_specs=[nat_spec] * 3 + [sorted_spec] * 3 + [nat_spec] * 4,
        scratch_shapes=[pltpu.VMEM((IN_ROWS, PAIR), F32), pltpu.VMEM((IN_ROWS, PAIR), F32)],
        compiler_params=_params(1),
    )(x, norm_g, w_in, *tabs)


def _stack_heads(q):
    even = lax.broadcasted_iota(jnp.int32, q.shape, 1) < HEAD_DIM
    zero = jnp.zeros_like(q)
    return jnp.concatenate([jnp.where(even, q, zero), jnp.where(even, zero, q)], axis=0)


def _pair_select(top_bottom):
    lane = lax.broadcasted_iota(jnp.int32, (BLK, PAIR), 1)
    return jnp.where(lane < HEAD_DIM, top_bottom[:BLK], top_bottom[BLK:])


def _pair_bcast(col):
    lane = lax.broadcasted_iota(jnp.int32, (BLK, PAIR), 1)
    return jnp.where(lane < HEAD_DIM, col[:BLK], col[BLK:])


def _dil_kernel(q_ref, k_ref, v_ref, cap_ref, mult_ref, o_ref, lse_ref):
    r4 = pl.program_id(1)
    ones = jnp.ones((4 * BLK, PAIR), BF16)
    for p in range(N_PAIRS):
        kcat = jnp.concatenate([k_ref[0, p, ap, 0] for ap in range(4)], axis=0)
        vext = jnp.concatenate(
            [jnp.concatenate([v_ref[0, p, ap, 0] for ap in range(4)], axis=0), ones], axis=1)
        for a0 in range(0, 4, DIL_GROUP):
            group = range(a0, a0 + DIL_GROUP)
            trows = slice(2 * a0 * BLK, 2 * (a0 + DIL_GROUP) * BLK)
            lhs = jnp.concatenate([_stack_heads(q_ref[0, p, a, 0]) for a in group], axis=0)
            s = jnp.minimum(_dot_nt(lhs, kcat), cap_ref[trows, :])
            m = jnp.max(s, axis=-1, keepdims=True)
            e = jnp.exp2(s - m)
            tile = lambda n, ap: e[2 * n * BLK:2 * (n + 1) * BLK, ap * BLK:(ap + 1) * BLK]
            pm = jnp.concatenate([
                jnp.concatenate([
                    tile(n, ap) * mult_ref[...] if ap == a else tile(n, ap)
                    for ap in range(4)], axis=1)
                for n, a in enumerate(group)], axis=0)
            acc = _dot(pm.astype(BF16), vext)
            den = acc[:, PAIR:]
            out = acc[:, :PAIR] / den
            lse = m + jnp.log(den) * LOG2_E
            for n, a in enumerate(group):
                rows = pl.ds(4 * a + r4, BLK, stride=DIL)
                o_ref[0, p, rows, :] = _pair_select(out[2 * n * BLK:2 * (n + 1) * BLK])
                lse_ref[0, p, rows, :] = _pair_select(lse[2 * n * BLK:2 * (n + 1) * BLK])


def _dil_tables():
    i = (np.arange(2 * BLK) % BLK)[:, None]
    j = np.arange(BLK)[None, :]
    cap = np.empty((4, 2 * BLK, 4, BLK), np.float32)
    for a in range(4):
        for ap in range(4):
            delta = 4 * (i - j) + (a - ap)
            in_d4 = (delta >= 0) & (delta <= BLK)
            in_d16 = (j <= i) & (a == ap)
            cap[a, :, ap, :] = np.where(in_d4 | in_d16, np.finfo(np.float32).max, NEG_INF)
    both = (i - j >= 0) & (4 * (i - j) <= BLK)
    mult = np.where(both, 2.0, 1.0).astype(np.float32)
    return jnp.asarray(cap.reshape(8 * BLK, 4 * BLK)), jnp.asarray(mult)


def _dil(q, k, v, cap, mult):
    b, _, _, nq, _ = k.shape
    t = nq * DIL
    by_res = lambda arr: arr.reshape(b, N_PAIRS, 4, 4, nq, PAIR)
    res_spec = pl.BlockSpec((1, N_PAIRS, 4, 1, nq, PAIR), lambda i, j: (i, 0, 0, j, 0, 0))
    nat_spec = pl.BlockSpec((1, N_PAIRS, t, PAIR), lambda i, j: (i, 0, 0, 0))
    nat_shape = jax.ShapeDtypeStruct((b, N_PAIRS, t, PAIR), F32)
    return pl.pallas_call(
        _dil_kernel,
        out_shape=[nat_shape, nat_shape],
        grid=(b, 4),
        in_specs=[res_spec, res_spec, res_spec,
                  pl.BlockSpec(cap.shape, lambda i, j: (0, 0)),
                  pl.BlockSpec(mult.shape, lambda i, j: (0, 0))],
        out_specs=[nat_spec, nat_spec],
        compiler_params=_params(2),
    )(by_res(q), by_res(k), by_res(v), cap, mult)


def _hyb_main_kernel(x_ref, g_ref, q_ref, k_ref, v_ref, o23_ref, l23_ref,
                     qr_ref, kr_ref, vr_ref, gr_ref,
                     cap_ref, decay_ref, zeta_ref, xi_ref, cd_ref, wout_ref,
                     o_ref, state_ref, m_scr):
    step = pl.program_id(1)

    @pl.when(step == 0)
    def _():
        state_ref[...] = jnp.zeros(state_ref.shape, F32)

    pairs = range(N_PAIRS)
    stack = lambda vals: jnp.concatenate(vals, axis=0)
    piece = lambda val, p, n: val[p * n:(p + 1) * n]
    lane = lax.broadcasted_iota(jnp.int32, (N_PAIRS * BLK, PAIR), 1)
    even = lane < HEAD_DIM
    row = lax.broadcasted_iota(jnp.int32, (N_PAIRS * BLK, PAIR), 0)
    same_head = lax.shift_right_logical(row & (PAIR - 1), 6) == lax.shift_right_logical(lane, 6)
    ones = jnp.ones((2 * BLK, PAIR), BF16)
    for c in range(MAIN_ROWS // BLK):
        rows = slice(c * BLK, (c + 1) * BLK)
        nb = step * (MAIN_ROWS // BLK) + c
        kstart = pl.multiple_of(jnp.maximum(nb - 1, 0) * BLK, BLK)
        cap = cap_ref[jnp.minimum(nb, 1)]
        s = stack([
            _dot_nt(_stack_heads(q_ref[0, p, rows, :]), k_ref[0, p, pl.ds(kstart, 2 * BLK), :])
            for p in pairs])
        s = jnp.minimum(s, cap)
        m1 = jnp.max(s, axis=-1, keepdims=True)
        e = jnp.exp2(s - m1).astype(BF16)
        acc = [_dot(piece(e, p, 2 * BLK),
                    jnp.concatenate([v_ref[0, p, pl.ds(kstart, 2 * BLK), :], ones], axis=1))
               for p in pairs]
        num1 = stack([_pair_select(acc[p][:, :PAIR]) for p in pairs])
        den1 = stack([_pair_select(acc[p][:, PAIR:]) for p in pairs])
        m1 = stack([_pair_bcast(piece(m1, p, 2 * BLK)) for p in pairs])
        l23 = stack([l23_ref[0, p, rows, :] for p in pairs])
        o23 = stack([o23_ref[0, p, rows, :] for p in pairs])
        top = jnp.maximum(l23, m1)
        w23 = jnp.exp2(l23 - top)
        w1 = jnp.exp2(m1 - top)
        attn = ((o23 * w23 + num1 * w1) / (w23 + den1 * w1)).astype(BF16)
        qst = [_stack_heads(qr_ref[0, p, rows, :]) for p in pairs]
        kr = stack([kr_ref[0, p, rows, :] for p in pairs])
        vr = [vr_ref[0, p, rows, :] for p in pairs]
        kr16 = kr.astype(BF16)
        scores = stack([_dot_nt(qst[p], piece(kr16, p, BLK)) for p in pairs]) * decay_ref[...]
        scores = scores.astype(BF16)
        inner = stack([_pair_select(_dot(piece(scores, p, 2 * BLK), vr[p])) for p in pairs])
        state = state_ref[...]
        state16 = state.astype(BF16)
        cross2 = [_dot(qst[p], piece(state16, p, PAIR)) for p in pairs]
        cross = stack([c2[:BLK] + c2[BLK:] for c2 in cross2]) * xi_ref[...]
        kz = kr * zeta_ref[...]
        kv = stack([_dot(piece(kz, p, BLK).T.astype(BF16), vr[p]) for p in pairs])
        state_ref[...] = state * cd_ref[...] + jnp.where(same_head, kv, 0.0)
        out = inner + cross
        sq = out * out
        ms_e = jnp.sum(jnp.where(even, sq, 0.0), axis=-1, keepdims=True)
        ms_o = jnp.sum(jnp.where(even, 0.0, sq), axis=-1, keepdims=True)
        ms = jnp.where(even, ms_e, ms_o) * (1.0 / HEAD_DIM)
        out = out * lax.rsqrt(ms + EPS)
        gate = stack([gr_ref[0, p, rows, :] for p in pairs])
        ret = ((gate * jax.nn.sigmoid(gate)) * out).astype(BF16)
        for p in pairs:
            m_scr[rows, p * PAIR:(p + 1) * PAIR] = piece(attn, p, BLK)
            m_scr[rows, MIX_W + p * PAIR:MIX_W + (p + 1) * PAIR] = piece(ret, p, BLK)
    y = _dot(m_scr[...], wout_ref[...])
    o_ref[0] = x_ref[0] + _rms(y, g_ref[3:4, :])


def _window_cap():
    i = (np.arange(2 * BLK) % BLK)[:, None]
    j = np.arange(2 * BLK)[None, :]
    fmax = np.finfo(np.float32).max
    delta = np.stack([i - j, BLK + i - j])
    cap = np.where((delta >= 0) & (delta <= BLK), fmax, NEG_INF).astype(np.float32)
    return jnp.asarray(np.tile(cap, (1, N_PAIRS, 1)))


def _hyb_main(x, norm_g, q, k, v, o23, l23, qr, kr, vr, gr, cap, consts, w_out, layer, j):
    b, t, d = x.shape
    n_g = norm_g.shape[1]
    blk = pl.BlockSpec((1, N_PAIRS, MAIN_ROWS, PAIR), lambda i, s: (i, 0, s, 0))
    full = pl.BlockSpec((1, N_PAIRS, t, PAIR), lambda i, s: (i, 0, 0, 0))
    xblk = pl.BlockSpec((1, MAIN_ROWS, d), lambda i, s: (i, s, 0))
    const3 = lambda arr: pl.BlockSpec(arr.shape, lambda i, s: (0,) * arr.ndim)
    decay, zeta, xi, cd = consts
    return pl.pallas_call(
        _hyb_main_kernel,
        out_shape=jax.ShapeDtypeStruct((b, t, d), F32),
        grid=(b, t // MAIN_ROWS),
        in_specs=[
            xblk,
            pl.BlockSpec((None, n_g, d), lambda i, s: (layer, 0, 0)),
            blk, full, full, blk, blk,
            blk, blk, blk, blk,
            const3(cap), const3(decay), const3(zeta), const3(xi), const3(cd),
            pl.BlockSpec((None, d, d), lambda i, s: (j, 0, 0)),
        ],
        out_specs=xblk,
        scratch_shapes=[pltpu.VMEM((N_PAIRS * PAIR, PAIR), F32),
                        pltpu.VMEM((MAIN_ROWS, d), BF16)],
        compiler_params=_params(2),
    )(x, norm_g, q, k, v, o23, l23, qr, kr, vr, gr, cap, decay, zeta, xi, cd, w_out)


def _rope_table(t, rot_dims, theta):
    half = rot_dims // 2
    inv = theta ** (-(jnp.arange(half, dtype=F32) * 2.0 / rot_dims))
    ang = jnp.arange(t, dtype=F32)[:, None] * inv[None, :]
    cos, sin = jnp.cos(ang), jnp.sin(ang)
    rest = HEAD_DIM - rot_dims
    cos64 = jnp.concatenate([cos, cos, jnp.ones((t, rest), F32)], axis=-1)
    sin64 = jnp.concatenate([-sin, sin, jnp.zeros((t, rest), F32)], axis=-1)
    return jnp.tile(cos64, (1, 2)), jnp.tile(sin64, (1, 2))


def _retention_consts(n_heads):
    c = BLK
    log_g = jnp.log(1.0 - jnp.exp2(-5.0 - jnp.arange(n_heads, dtype=F32)))
    idx = jnp.arange(c, dtype=F32)
    diff = idx[:, None] - idx[None, :]
    decay = jnp.where(diff >= 0, jnp.exp(log_g[:, None, None] * jnp.maximum(diff, 0.0)), 0.0)
    zeta = jnp.exp(log_g[:, None] * (c - 1.0 - idx)[None, :])
    xi = jnp.exp(log_g[:, None] * (idx + 1.0)[None, :])
    chunk_decay = jnp.exp(log_g * c)
    per_lane = lambda hc: jnp.repeat(
        hc.reshape(N_PAIRS, 2, -1).transpose(0, 2, 1), HEAD_DIM, axis=-1).reshape(-1, PAIR)
    decay_st = decay.reshape(2 * N_PAIRS * c, c)
    chunk_decay = jnp.broadcast_to(chunk_decay[:, None], (n_heads, PAIR))
    return decay_st, per_lane(zeta), per_lane(xi), per_lane(chunk_decay)


def kernel(x, norm_g, ffn_w_gate, ffn_w_up, ffn_w_down, hyb_w_in, hyb_w_out,
           gmlp_w_in, gmlp_ln_g, gmlp_ln_b, gmlp_w_s, gmlp_b_s, gmlp_w_out):
    b, t, d = x.shape
    depth = norm_g.shape[0]
    wg, wu, wd = ffn_w_gate, ffn_w_up, ffn_w_down
    hyb_in, hyb_out = hyb_w_in.astype(BF16), hyb_w_out.astype(BF16)
    g_in, g_out = gmlp_w_in.astype(BF16), gmlp_w_out.astype(BF16)
    tabs = _rope_table(t, ROPE_DIMS, ROPE_THETA) + _rope_table(t, HEAD_DIM, RET_ROPE_THETA)
    consts = _retention_consts(2 * N_PAIRS)
    dil_cap, dil_mult = _dil_tables()
    win_cap = _window_cap()
    bias_full = jnp.repeat(jnp.swapaxes(gmlp_b_s, 1, 2), BLK, axis=2)
    ln_g, ln_b = gmlp_ln_g[:, None, :], gmlp_ln_b[:, None, :]
    x = x.reshape(b * t, d)
    for layer in range(depth):
        j = layer // 2
        x = _ffn(x, norm_g, wg, wu, wd, layer, 0)
        if layer % 2 == 0:
            qa, ka, va, q16, k16, v16, qr, kr, vr, gr = _hyb_in(
                x, norm_g, hyb_in, tabs, layer, j, b, t)
            o23, l23 = _dil(q16, k16, v16, dil_cap, dil_mult)
            x = _hyb_main(x.reshape(b, t, d), norm_g, qa, ka, va, o23, l23,
                          qr, kr, vr, gr, win_cap, consts, hyb_out, layer, j
                          ).reshape(b * t, d)
        else:
            x = _gmlp(x, norm_g, g_in, ln_g, ln_b, gmlp_w_s, bias_full, g_out, layer, j)
        x = _ffn(x, norm_g, wg, wu, wd, layer, 1)
    return x.reshape(b, t, d)
```

```python
import functools

import jax
import jax.numpy as jnp
import numpy as np
from jax import lax
from jax.experimental import pallas as pl
from jax.experimental.pallas import tpu as pltpu

F32 = jnp.float32
BF16 = jnp.bfloat16

D_MODEL = 1024
HEAD_DIM = 64
PAIR = 2 * HEAD_DIM
N_PAIRS = 4
MIX_W = N_PAIRS * PAIR
BLK = 128
DIL = 16
D_FF = 2816
FF_CHUNK = 256
GMLP_GROUPS = 8
ROPE_THETA = 500000.0
ROPE_DIMS = HEAD_DIM // 4
RET_ROPE_THETA = 10000.0
EPS = 1e-6
NEG_INF = -1e30
LOG2_E = np.float32(np.log2(np.e))
QK_SCALE = np.float32(HEAD_DIM ** -0.5 * np.log2(np.e))

FFN_ROWS = 512
FFN_NORM_GROUPS = 8
GMLP_ROWS = 512
IN_ROWS = 1024
MAIN_ROWS = 1024
DIL_RES = 4
DIL_GROUP = 2
VMEM_LIMIT = 60 * 1024 * 1024


def _dot(a, b):
    return jnp.dot(a, b, preferred_element_type=F32)


def _dot_nt(a, b):
    return lax.dot_general(a, b, (((1,), (1,)), ((), ())), preferred_element_type=F32)


def _rms(x, g):
    y = x * lax.rsqrt(jnp.mean(x * x, axis=-1, keepdims=True) + EPS)
    return y * g


def _gelu(x):
    return 0.5 * x * (1.0 + lax.erf(x * np.float32(np.sqrt(0.5))))


def _params(n_axes):
    return pltpu.CompilerParams(
        dimension_semantics=("arbitrary",) * n_axes, vmem_limit_bytes=VMEM_LIMIT)


def _zero_from(*vals):
    flags = None
    for v in vals:
        for r in range(0, v.shape[0], 8):
            for l in range(0, v.shape[1], PAIR):
                f = jnp.where(v[r:r + 8, l:l + PAIR] > 0.0, 1, 0)
                flags = f if flags is None else flags | f
    zero = lax.shift_right_logical(flags, 1).astype(F32)[0:1, :]
    return jnp.concatenate([zero] * (FF_CHUNK // PAIR), axis=1)


def _ffn_kernel(xprev_ref, xnext_ref, g_ref, wg_hbm, wu_hbm, wd_hbm, o_ref,
                h_scr, acc_scr, wg_ref, wu_ref, wd_ref, sem, *, g_row, n_tiles, layer, which):
    i = pl.program_id(0)
    g_in = g_ref[g_row:g_row + 1, :]
    g_out = g_ref[g_row + 1:g_row + 2, :]
    n_chunks = D_FF // FF_CHUNK

    def weight_copies(c):
        sl = slice(c * FF_CHUNK, (c + 1) * FF_CHUNK)
        return [
            pltpu.make_async_copy(wg_hbm.at[layer, which, :, sl], wg_ref.at[:, sl], sem.at[0, c]),
            pltpu.make_async_copy(wu_hbm.at[layer, which, :, sl], wu_ref.at[:, sl], sem.at[1, c]),
            pltpu.make_async_copy(wd_hbm.at[layer, which, sl, :], wd_ref.at[sl, :], sem.at[2, c]),
        ]

    def finish(rs):
        out_prev = xprev_ref[rs, :] + 0.5 * _rms(acc_scr[rs, :], g_out)
        o_ref[rs, :] = out_prev
        return out_prev

    def step(wait_weights):
        h = h_scr[...]
        zero_rows = []
        group = FFN_ROWS // FFN_NORM_GROUPS
        for j in range(FFN_NORM_GROUPS):
            rs = slice(j * group, (j + 1) * group)
            h_next = _rms(xnext_ref[rs, :], g_in)
            h_scr[rs, :] = h_next.astype(BF16)
            zero_rows.append(_zero_from(h_next, finish(rs)))
        acc = jnp.zeros(acc_scr.shape, F32)
        for c in range(n_chunks):
            sl = slice(c * FF_CHUNK, (c + 1) * FF_CHUNK)
            if wait_weights:
                for copy in weight_copies(c):
                    copy.wait()
            gate = _dot(h, wg_ref[:, sl].astype(BF16))
            up = _dot(h, wu_ref[:, sl].astype(BF16))
            if 1 <= c <= FFN_NORM_GROUPS:
                up = up + zero_rows[c - 1]
            act = (gate * jax.nn.sigmoid(gate)) * up
            acc = acc + _dot(act.astype(BF16), wd_ref[sl, :].astype(BF16))
        acc_scr[...] = acc

    @pl.when(i == 0)
    def _():
        for c in range(n_chunks):
            for copy in weight_copies(c):
                copy.start()
        h_scr[...] = _rms(xprev_ref[...], g_in).astype(BF16)
        acc_scr[...] = jnp.zeros(acc_scr.shape, F32)
        step(wait_weights=True)

    @pl.when(jnp.logical_and(i > 0, i < n_tiles))
    def _():
        step(wait_weights=False)

    @pl.when(i == n_tiles)
    def _():
        finish(slice(None))


def _ffn(x, norm_g, wg, wu, wd, layer, which):
    n, d = x.shape
    n_g = norm_g.shape[1]
    n_tiles = n // FFN_ROWS
    prev_tile = lambda i: (jnp.maximum(i - 1, 0), 0)
    next_tile = lambda i: (jnp.minimum(i + 1, n_tiles - 1), 0)
    in_hbm = pl.BlockSpec(memory_space=pl.ANY)
    return pl.pallas_call(
        functools.partial(_ffn_kernel, g_row=4 * which, n_tiles=n_tiles, layer=layer, which=which),
        out_shape=jax.ShapeDtypeStruct((n, d), F32),
        grid=(n_tiles + 1,),
        in_specs=[
            pl.BlockSpec((FFN_ROWS, d), prev_tile),
            pl.BlockSpec((FFN_ROWS, d), next_tile),
            pl.BlockSpec((None, n_g, d), lambda i: (layer, 0, 0)),
            in_hbm, in_hbm, in_hbm,
        ],
        out_specs=pl.BlockSpec((FFN_ROWS, d), prev_tile),
        scratch_shapes=[pltpu.VMEM((FFN_ROWS, d), BF16), pltpu.VMEM((FFN_ROWS, d), F32),
                        pltpu.VMEM((d, D_FF), F32), pltpu.VMEM((d, D_FF), F32),
                        pltpu.VMEM((D_FF, d), F32),
                        pltpu.SemaphoreType.DMA((3, D_FF // FF_CHUNK))],
        compiler_params=_params(1),
    )(x, x, norm_g, wg, wu, wd)


def _gmlp_kernel(x_ref, g_ref, win_ref, lng_ref, lnb_ref, ws_ref, bias_ref, wout_ref,
                 o_ref, m_scr):
    x = x_ref[...]
    rows, d = x.shape
    h = _rms(x, g_ref[2:3, :]).astype(BF16)
    chunk = lambda c0: _gelu(_dot(h, win_ref[:, c0:c0 + FF_CHUNK]))
    v = jnp.concatenate([chunk(d + c * FF_CHUNK) for c in range(d // FF_CHUNK)], axis=1)
    u = jnp.concatenate([chunk(c * FF_CHUNK) for c in range(d // FF_CHUNK)], axis=1)
    mu = jnp.mean(v, axis=-1, keepdims=True)
    var = jnp.mean(jnp.square(v - mu), axis=-1, keepdims=True)
    v = ((v - mu) * lax.rsqrt(var + EPS)) * lng_ref[...] + lnb_ref[...]
    ii = lax.broadcasted_iota(jnp.int32, (BLK, BLK), 0)
    jj = lax.broadcasted_iota(jnp.int32, (BLK, BLK), 1)
    causal = jj <= ii
    for grp in range(GMLP_GROUPS):
        lanes = slice(grp * BLK, (grp + 1) * BLK)
        w = jnp.where(causal, ws_ref[grp], 0.0).astype(BF16)
        for c in range(rows // BLK):
            rs = slice(c * BLK, (c + 1) * BLK)
            s = _dot(w, v[rs, lanes].astype(BF16)) + bias_ref[:, lanes]
            m_scr[rs, lanes] = (u[rs, lanes] * s).astype(BF16)
    y = _dot(m_scr[...], wout_ref[...])
    o_ref[...] = x + _rms(y, g_ref[3:4, :])


def _gmlp(x, norm_g, w_in, ln_g, ln_b, w_s, bias_full, w_out, layer, j):
    n, d = x.shape
    n_g = norm_g.shape[1]
    sel = lambda i: (j, 0, 0)
    return pl.pallas_call(
        _gmlp_kernel,
        out_shape=jax.ShapeDtypeStruct((n, d), F32),
        grid=(n // GMLP_ROWS,),
        in_specs=[
            pl.BlockSpec((GMLP_ROWS, d), lambda i: (i, 0)),
            pl.BlockSpec((None, n_g, d), lambda i: (layer, 0, 0)),
            pl.BlockSpec((None, d, 2 * d), sel),
            pl.BlockSpec((None, 1, d), sel),
            pl.BlockSpec((None, 1, d), sel),
            pl.BlockSpec((None, GMLP_GROUPS, BLK, BLK), lambda i: (j, 0, 0, 0)),
            pl.BlockSpec((None, BLK, d), sel),
            pl.BlockSpec((None, d, d), sel),
        ],
        out_specs=pl.BlockSpec((GMLP_ROWS, d), lambda i: (i, 0)),
        scratch_shapes=[pltpu.VMEM((GMLP_ROWS, d), BF16)],
        compiler_params=_params(1),
    )(x, norm_g, w_in, ln_g, ln_b, w_s, bias_full, w_out)


def _rope(z, cos, sin, half):
    lane = lax.broadcasted_iota(jnp.int32, z.shape, 1)
    up = pltpu.roll(z, PAIR - half, axis=1)
    dn = pltpu.roll(z, half, axis=1)
    partner = jnp.where((lane & (HEAD_DIM - 1)) < half, up, dn)
    return z * cos + partner * sin


def _hyb_in_kernel(x_ref, g_ref, w_ref, ca_ref, sa_ref, cr_ref, sr_ref,
                   qa_ref, ka_ref, va_ref, q16_ref, k16_ref, v16_ref,
                   qr_ref, kr_ref, vr_ref, gr_ref, sort_scr, sort2_scr):
    h = _rms(x_ref[...], g_ref[2:3, :]).astype(BF16)
    ca, sa, cr, sr = ca_ref[...], sa_ref[...], cr_ref[...], sr_ref[...]

    def proj(group, half):
        c0 = group * MIX_W + half * 2 * PAIR
        z = _dot(h, w_ref[:, c0:c0 + 2 * PAIR])
        return [(2 * half, z[:, :PAIR]), (2 * half + 1, z[:, PAIR:])]

    def sorted_rows(val):
        sort_scr[...] = val
        quarter = IN_ROWS // 4
        for r4 in range(4):
            sort2_scr[r4 * quarter:(r4 + 1) * quarter, :] = sort_scr[pl.ds(r4, quarter, stride=4), :]
        return [sort2_scr[pl.ds((r % 4) * quarter + r // 4, IN_ROWS // DIL, stride=4), :]
                for r in range(DIL)]

    for half in range(2):
        for p, z in proj(0, half):
            qa = _rope(z, ca, sa, ROPE_DIMS // 2) * QK_SCALE
            qa_ref[0, p] = qa.astype(BF16)
            for r, rows in enumerate(sorted_rows(qa)):
                q16_ref[0, p, r] = rows.astype(BF16)
        for p, z in proj(3, half):
            qr_ref[0, p] = _rope(z, cr, sr, HEAD_DIM // 2).astype(BF16)
        for p, z in proj(1, half):
            ka = _rope(z, ca, sa, ROPE_DIMS // 2)
            ka_ref[0, p] = ka.astype(BF16)
            for r, rows in enumerate(sorted_rows(ka)):
                k16_ref[0, p, r] = rows.astype(BF16)
        for p, z in proj(4, half):
            kr_ref[0, p] = _rope(z, cr, sr, HEAD_DIM // 2) * (HEAD_DIM ** -0.5)
        for p, z in proj(2, half):
            va_ref[0, p] = z.astype(BF16)
            for r, rows in enumerate(sorted_rows(z)):
                v16_ref[0, p, r] = rows.astype(BF16)
        for p, z in proj(5, half):
            vr_ref[0, p] = z.astype(BF16)
        for p, z in proj(6, half):
            gr_ref[0, p] = z


def _hyb_in(x, norm_g, w_in, tabs, layer, j, b, t):
    n, d = x.shape
    n_g = norm_g.shape[1]
    per_b = t // IN_ROWS
    nat_spec = pl.BlockSpec((1, N_PAIRS, IN_ROWS, PAIR), lambda i: (i // per_b, 0, i % per_b, 0))
    sorted_spec = pl.BlockSpec((1, N_PAIRS, DIL, IN_ROWS // DIL, PAIR),
                               lambda i: (i // per_b, 0, 0, i % per_b, 0))
    tab_spec = pl.BlockSpec((IN_ROWS, PAIR), lambda i: (i % per_b, 0))
    nat = lambda dt: jax.ShapeDtypeStruct((b, N_PAIRS, t, PAIR), dt)
    srt = jax.ShapeDtypeStruct((b, N_PAIRS, DIL, t // DIL, PAIR), BF16)
    return pl.pallas_call(
        _hyb_in_kernel,
        out_shape=[nat(BF16)] * 3 + [srt] * 3 + [nat(BF16), nat(F32), nat(BF16), nat(F32)],
        grid=(n // IN_ROWS,),
        in_specs=[
            pl.BlockSpec((IN_ROWS, d), lambda i: (i, 0)),
            pl.BlockSpec((None, n_g, d), lambda i: (layer, 0, 0)),
            pl.BlockSpec((None,) + w_in.shape[1:], lambda i: (j, 0, 0)),
            tab_spec, tab_spec, tab_spec, tab_spec,
        ],
        out_specs=[nat_spec] * 3 + [sorted_spec] * 3 + [nat_spec] * 4,
        scratch_shapes=[pltpu.VMEM((IN_ROWS, PAIR), F32), pltpu.VMEM((IN_ROWS, PAIR), F32)],
        compiler_params=_params(1),
    )(x, norm_g, w_in, *tabs)


def _stack_heads(q):
    even = lax.broadcasted_iota(jnp.int32, q.shape, 1) < HEAD_DIM
    zero = jnp.zeros_like(q)
    return jnp.concatenate([jnp.where(even, q, zero), jnp.where(even, zero, q)], axis=0)


def _pair_select(top_bottom):
    lane = lax.broadcasted_iota(jnp.int32, (BLK, PAIR), 1)
    return jnp.where(lane < HEAD_DIM, top_bottom[:BLK], top_bottom[BLK:])


def _pair_bcast(col):
    lane = lax.broadcasted_iota(jnp.int32, (BLK, PAIR), 1)
    return jnp.where(lane < HEAD_DIM, col[:BLK], col[BLK:])


def _dil_kernel(q_ref, k_ref, v_ref, cap_ref, mult_ref, o_ref, lse_ref):
    ones = jnp.ones((4 * BLK, PAIR), BF16)
    for res, p in [(res, p) for res in range(DIL_RES) for p in range(N_PAIRS)]:
        r4 = pl.program_id(1) * DIL_RES + res
        kcat = jnp.concatenate([k_ref[0, p, ap, res] for ap in range(4)], axis=0)
        vext = jnp.concatenate(
            [jnp.concatenate([v_ref[0, p, ap, res] for ap in range(4)], axis=0), ones], axis=1)
        for a0 in range(0, 4, DIL_GROUP):
            group = range(a0, a0 + DIL_GROUP)
            trows = slice(2 * a0 * BLK, 2 * (a0 + DIL_GROUP) * BLK)
            lhs = jnp.concatenate([_stack_heads(q_ref[0, p, a, res]) for a in group], axis=0)
            s = jnp.minimum(_dot_nt(lhs, kcat), cap_ref[trows, :])
            m = jnp.max(s, axis=-1, keepdims=True)
            e = jnp.exp2(s - m)
            tile = lambda n, ap: e[2 * n * BLK:2 * (n + 1) * BLK, ap * BLK:(ap + 1) * BLK]
            pm = jnp.concatenate([
                jnp.concatenate([
                    tile(n, ap) * mult_ref[...] if ap == a else tile(n, ap)
                    for ap in range(4)], axis=1)
                for n, a in enumerate(group)], axis=0)
            acc = _dot(pm.astype(BF16), vext)
            den = acc[:, PAIR:]
            out = acc[:, :PAIR] / den
            lse = m + jnp.log(den) * LOG2_E
            for n, a in enumerate(group):
                rows = pl.ds(4 * a + r4, BLK, stride=DIL)
                o_ref[0, p, rows, :] = _pair_select(out[2 * n * BLK:2 * (n + 1) * BLK])
                lse_ref[0, p, rows, :] = _pair_select(lse[2 * n * BLK:2 * (n + 1) * BLK])


def _dil_tables():
    i = (np.arange(2 * BLK) % BLK)[:, None]
    j = np.arange(BLK)[None, :]
    cap = np.empty((4, 2 * BLK, 4, BLK), np.float32)
    for a in range(4):
        for ap in range(4):
            delta = 4 * (i - j) + (a - ap)
            in_d4 = (delta >= 0) & (delta <= BLK)
            in_d16 = (j <= i) & (a == ap)
            cap[a, :, ap, :] = np.where(in_d4 | in_d16, np.finfo(np.float32).max, NEG_INF)
    both = (i - j >= 0) & (4 * (i - j) <= BLK)
    mult = np.where(both, 2.0, 1.0).astype(np.float32)
    return jnp.asarray(cap.reshape(8 * BLK, 4 * BLK)), jnp.asarray(mult)


def _dil(q, k, v, cap, mult):
    b, _, _, nq, _ = k.shape
    t = nq * DIL
    by_res = lambda arr: arr.reshape(b, N_PAIRS, 4, 4, nq, PAIR)
    res_spec = pl.BlockSpec((1, N_PAIRS, 4, DIL_RES, nq, PAIR), lambda i, j: (i, 0, 0, j, 0, 0))
    nat_spec = pl.BlockSpec((1, N_PAIRS, t, PAIR), lambda i, j: (i, 0, 0, 0))
    nat_shape = jax.ShapeDtypeStruct((b, N_PAIRS, t, PAIR), F32)
    return pl.pallas_call(
        _dil_kernel,
        out_shape=[nat_shape, nat_shape],
        grid=(b, 4 // DIL_RES),
        in_specs=[res_spec, res_spec, res_spec,
                  pl.BlockSpec(cap.shape, lambda i, j: (0, 0)),
                  pl.BlockSpec(mult.shape, lambda i, j: (0, 0))],
        out_specs=[nat_spec, nat_spec],
        compiler_params=_params(2),
    )(by_res(q), by_res(k), by_res(v), cap, mult)


def _hyb_main_kernel(x_ref, g_ref, q_ref, k_ref, v_ref, o23_ref, l23_ref,
                     qr_ref, kr_ref, vr_ref, gr_ref,
                     cap_ref, decay_ref, zeta_ref, xi_ref, cd_ref, wout_ref,
                     o_ref, state_ref, m_scr):
    step = pl.program_id(1)

    @pl.when(step == 0)
    def _():
        state_ref[...] = jnp.zeros(state_ref.shape, F32)

    pairs = range(N_PAIRS)
    stack = lambda vals: jnp.concatenate(vals, axis=0)
    piece = lambda val, p, n: val[p * n:(p + 1) * n]
    lane = lax.broadcasted_iota(jnp.int32, (N_PAIRS * BLK, PAIR), 1)
    even = lane < HEAD_DIM
    row = lax.broadcasted_iota(jnp.int32, (N_PAIRS * BLK, PAIR), 0)
    same_head = lax.shift_right_logical(row & (PAIR - 1), 6) == lax.shift_right_logical(lane, 6)
    ones = jnp.ones((2 * BLK, PAIR), BF16)
    for c in range(MAIN_ROWS // BLK):
        rows = slice(c * BLK, (c + 1) * BLK)
        nb = step * (MAIN_ROWS // BLK) + c
        kstart = pl.multiple_of(jnp.maximum(nb - 1, 0) * BLK, BLK)
        cap = cap_ref[jnp.minimum(nb, 1)]
        s = stack([
            _dot_nt(_stack_heads(q_ref[0, p, rows, :]), k_ref[0, p, pl.ds(kstart, 2 * BLK), :])
            for p in pairs])
        s = jnp.minimum(s, cap)
        m1 = jnp.max(s, axis=-1, keepdims=True)
        e = jnp.exp2(s - m1).astype(BF16)
        acc = [_dot(piece(e, p, 2 * BLK),
                    jnp.concatenate([v_ref[0, p, pl.ds(kstart, 2 * BLK), :], ones], axis=1))
               for p in pairs]
        num1 = stack([_pair_select(acc[p][:, :PAIR]) for p in pairs])
        den1 = stack([_pair_select(acc[p][:, PAIR:]) for p in pairs])
        m1 = stack([_pair_bcast(piece(m1, p, 2 * BLK)) for p in pairs])
        l23 = stack([l23_ref[0, p, rows, :] for p in pairs])
        o23 = stack([o23_ref[0, p, rows, :] for p in pairs])
        top = jnp.maximum(l23, m1)
        w23 = jnp.exp2(l23 - top)
        w1 = jnp.exp2(m1 - top)
        attn = ((o23 * w23 + num1 * w1) / (w23 + den1 * w1)).astype(BF16)
        qst = [_stack_heads(qr_ref[0, p, rows, :]) for p in pairs]
        kr = stack([kr_ref[0, p, rows, :] for p in pairs])
        vr = [vr_ref[0, p, rows, :] for p in pairs]
        kr16 = kr.astype(BF16)
        scores = stack([_dot_nt(qst[p], piece(kr16, p, BLK)) for p in pairs]) * decay_ref[...]
        scores = scores.astype(BF16)
        inner = stack([_pair_select(_dot(piece(scores, p, 2 * BLK), vr[p])) for p in pairs])
        state = state_ref[...]
        state16 = state.astype(BF16)
        cross2 = [_dot(qst[p], piece(state16, p, PAIR)) for p in pairs]
        cross = stack([c2[:BLK] + c2[BLK:] for c2 in cross2]) * xi_ref[...]
        kz = kr * zeta_ref[...]
        kv = stack([_dot(piece(kz, p, BLK).T.astype(BF16), vr[p]) for p in pairs])
        state_ref[...] = state * cd_ref[...] + jnp.where(same_head, kv, 0.0)
        out = inner + cross
        sq = out * out
        ms_e = jnp.sum(jnp.where(even, sq, 0.0), axis=-1, keepdims=True)
        ms_o = jnp.sum(jnp.where(even, 0.0, sq), axis=-1, keepdims=True)
        ms = jnp.where(even, ms_e, ms_o) * (1.0 / HEAD_DIM)
        out = out * lax.rsqrt(ms + EPS)
        gate = stack([gr_ref[0, p, rows, :] for p in pairs])
        ret = ((gate * jax.nn.sigmoid(gate)) * out).astype(BF16)
        for p in pairs:
            m_scr[rows, p * PAIR:(p + 1) * PAIR] = piece(attn, p, BLK)
            m_scr[rows, MIX_W + p * PAIR:MIX_W + (p + 1) * PAIR] = piece(ret, p, BLK)
    y = _dot(m_scr[...], wout_ref[...])
    o_ref[0] = x_ref[0] + _rms(y, g_ref[3:4, :])


def _window_cap():
    i = (np.arange(2 * BLK) % BLK)[:, None]
    j = np.arange(2 * BLK)[None, :]
    fmax = np.finfo(np.float32).max
    delta = np.stack([i - j, BLK + i - j])
    cap = np.where((delta >= 0) & (delta <= BLK), fmax, NEG_INF).astype(np.float32)
    return jnp.asarray(np.tile(cap, (1, N_PAIRS, 1)))


def _hyb_main(x, norm_g, q, k, v, o23, l23, qr, kr, vr, gr, cap, consts, w_out, layer, j):
    b, t, d = x.shape
    n_g = norm_g.shape[1]
    blk = pl.BlockSpec((1, N_PAIRS, MAIN_ROWS, PAIR), lambda i, s: (i, 0, s, 0))
    full = pl.BlockSpec((1, N_PAIRS, t, PAIR), lambda i, s: (i, 0, 0, 0))
    xblk = pl.BlockSpec((1, MAIN_ROWS, d), lambda i, s: (i, s, 0))
    const3 = lambda arr: pl.BlockSpec(arr.shape, lambda i, s: (0,) * arr.ndim)
    decay, zeta, xi, cd = consts
    return pl.pallas_call(
        _hyb_main_kernel,
        out_shape=jax.ShapeDtypeStruct((b, t, d), F32),
        grid=(b, t // MAIN_ROWS),
        in_specs=[
            xblk,
            pl.BlockSpec((None, n_g, d), lambda i, s: (layer, 0, 0)),
            blk, full, full, blk, blk,
            blk, blk, blk, blk,
            const3(cap), const3(decay), const3(zeta), const3(xi), const3(cd),
            pl.BlockSpec((None, d, d), lambda i, s: (j, 0, 0)),
        ],
        out_specs=xblk,
        scratch_shapes=[pltpu.VMEM((N_PAIRS * PAIR, PAIR), F32),
                        pltpu.VMEM((MAIN_ROWS, d), BF16)],
        compiler_params=_params(2),
    )(x, norm_g, q, k, v, o23, l23, qr, kr, vr, gr, cap, decay, zeta, xi, cd, w_out)


def _rope_table(t, rot_dims, theta):
    half = rot_dims // 2
    inv = theta ** (-(jnp.arange(half, dtype=F32) * 2.0 / rot_dims))
    ang = jnp.arange(t, dtype=F32)[:, None] * inv[None, :]
    cos, sin = jnp.cos(ang), jnp.sin(ang)
    rest = HEAD_DIM - rot_dims
    cos64 = jnp.concatenate([cos, cos, jnp.ones((t, rest), F32)], axis=-1)
    sin64 = jnp.concatenate([-sin, sin, jnp.zeros((t, rest), F32)], axis=-1)
    return jnp.tile(cos64, (1, 2)), jnp.tile(sin64, (1, 2))


def _retention_consts(n_heads):
    c = BLK
    log_g = jnp.log(1.0 - jnp.exp2(-5.0 - jnp.arange(n_heads, dtype=F32)))
    idx = jnp.arange(c, dtype=F32)
    diff = idx[:, None] - idx[None, :]
    decay = jnp.where(diff >= 0, jnp.exp(log_g[:, None, None] * jnp.maximum(diff, 0.0)), 0.0)
    zeta = jnp.exp(log_g[:, None] * (c - 1.0 - idx)[None, :])
    xi = jnp.exp(log_g[:, None] * (idx + 1.0)[None, :])
    chunk_decay = jnp.exp(log_g * c)
    per_lane = lambda hc: jnp.repeat(
        hc.reshape(N_PAIRS, 2, -1).transpose(0, 2, 1), HEAD_DIM, axis=-1).reshape(-1, PAIR)
    decay_st = decay.reshape(2 * N_PAIRS * c, c)
    chunk_decay = jnp.broadcast_to(chunk_decay[:, None], (n_heads, PAIR))
    return decay_st, per_lane(zeta), per_lane(xi), per_lane(chunk_decay)


def kernel(x, norm_g, ffn_w_gate, ffn_w_up, ffn_w_down, hyb_w_in, hyb_w_out,
           gmlp_w_in, gmlp_ln_g, gmlp_ln_b, gmlp_w_s, gmlp_b_s, gmlp_w_out):
    b, t, d = x.shape
    depth = norm_g.shape[0]
    wg, wu, wd = ffn_w_gate, ffn_w_up, ffn_w_down
    hyb_in, hyb_out = hyb_w_in.astype(BF16), hyb_w_out.astype(BF16)
    g_in, g_out = gmlp_w_in.astype(BF16), gmlp_w_out.astype(BF16)
    tabs = _rope_table(t, ROPE_DIMS, ROPE_THETA) + _rope_table(t, HEAD_DIM, RET_ROPE_THETA)
    consts = _retention_consts(2 * N_PAIRS)
    dil_cap, dil_mult = _dil_tables()
    win_cap = _window_cap()
    bias_full = jnp.repeat(jnp.swapaxes(gmlp_b_s, 1, 2), BLK, axis=2)
    ln_g, ln_b = gmlp_ln_g[:, None, :], gmlp_ln_b[:, None, :]
    x = x.reshape(b * t, d)
    for layer in range(depth):
        j = layer // 2
        x = _ffn(x, norm_g, wg, wu, wd, layer, 0)
        if layer % 2 == 0:
            qa, ka, va, q16, k16, v16, qr, kr, vr, gr = _hyb_in(
                x, norm_g, hyb_in, tabs, layer, j, b, t)
            o23, l23 = _dil(q16, k16, v16, dil_cap, dil_mult)
            x = _hyb_main(x.reshape(b, t, d), norm_g, qa, ka, va, o23, l23,
                          qr, kr, vr, gr, win_cap, consts, hyb_out, layer, j
                          ).reshape(b * t, d)
        else:
            x = _gmlp(x, norm_g, g_in, ln_g, ln_b, gmlp_w_s, bias_full, g_out, layer, j)
        x = _ffn(x, norm_g, wg, wu, wd, layer, 1)
    return x.reshape(b, t, d)
```

```python
import functools

import jax
import jax.numpy as jnp
import numpy as np
from jax import lax
from jax.experimental import pallas as pl
from jax.experimental.pallas import tpu as pltpu

F32 = jnp.float32
BF16 = jnp.bfloat16

D_MODEL = 1024
HEAD_DIM = 64
PAIR = 2 * HEAD_DIM
N_PAIRS = 4
MIX_W = N_PAIRS * PAIR
BLK = 128
DIL = 16
D_FF = 2816
FF_CHUNK = 256
GMLP_GROUPS = 8
ROPE_THETA = 500000.0
ROPE_DIMS = HEAD_DIM // 4
RET_ROPE_THETA = 10000.0
EPS = 1e-6
NEG_INF = -1e30
LOG2_E = np.float32(np.log2(np.e))
QK_SCALE = np.float32(HEAD_DIM ** -0.5 * np.log2(np.e))

FFN_ROWS = 512
FFN_NORM_GROUPS = 8
GMLP_ROWS = 1024
IN_ROWS = 1024
MAIN_ROWS = 1024
DIL_RES = 4
DIL_GROUP = 2
VMEM_LIMIT = 60 * 1024 * 1024


def _dot(a, b):
    return jnp.dot(a, b, preferred_element_type=F32)


def _dot_nt(a, b):
    return lax.dot_general(a, b, (((1,), (1,)), ((), ())), preferred_element_type=F32)


def _rms(x, g):
    y = x * lax.rsqrt(jnp.mean(x * x, axis=-1, keepdims=True) + EPS)
    return y * g


def _gelu(x):
    return 0.5 * x * (1.0 + lax.erf(x * np.float32(np.sqrt(0.5))))


def _params(n_axes):
    return pltpu.CompilerParams(
        dimension_semantics=("arbitrary",) * n_axes, vmem_limit_bytes=VMEM_LIMIT)


def _zero_from(*vals):
    flags = None
    for v in vals:
        for r in range(0, v.shape[0], 8):
            for l in range(0, v.shape[1], PAIR):
                f = jnp.where(v[r:r + 8, l:l + PAIR] > 0.0, 1, 0)
                flags = f if flags is None else flags | f
    zero = lax.shift_right_logical(flags, 1).astype(F32)[0:1, :]
    return jnp.concatenate([zero] * (FF_CHUNK // PAIR), axis=1)


def _ffn_kernel(xprev_ref, xnext_ref, g_ref, wg_hbm, wu_hbm, wd_hbm, o_ref,
                h_scr, acc_scr, wg_ref, wu_ref, wd_ref, sem, *, g_row, n_tiles, layer, which):
    i = pl.program_id(0)
    g_in = g_ref[g_row:g_row + 1, :]
    g_out = g_ref[g_row + 1:g_row + 2, :]
    n_chunks = D_FF // FF_CHUNK

    def weight_copies(c):
        sl = slice(c * FF_CHUNK, (c + 1) * FF_CHUNK)
        return [
            pltpu.make_async_copy(wg_hbm.at[layer, which, :, sl], wg_ref.at[:, sl], sem.at[0, c]),
            pltpu.make_async_copy(wu_hbm.at[layer, which, :, sl], wu_ref.at[:, sl], sem.at[1, c]),
            pltpu.make_async_copy(wd_hbm.at[layer, which, sl, :], wd_ref.at[sl, :], sem.at[2, c]),
        ]

    def finish(rs):
        out_prev = xprev_ref[rs, :] + 0.5 * _rms(acc_scr[rs, :], g_out)
        o_ref[rs, :] = out_prev
        return out_prev

    def step(wait_weights):
        h = h_scr[...]
        zero_rows = []
        group = FFN_ROWS // FFN_NORM_GROUPS
        for j in range(FFN_NORM_GROUPS):
            rs = slice(j * group, (j + 1) * group)
            h_next = _rms(xnext_ref[rs, :], g_in)
            h_scr[rs, :] = h_next.astype(BF16)
            zero_rows.append(_zero_from(h_next, finish(rs)))
        acc = jnp.zeros(acc_scr.shape, F32)
        for c in range(n_chunks):
            sl = slice(c * FF_CHUNK, (c + 1) * FF_CHUNK)
            if wait_weights:
                for copy in weight_copies(c):
                    copy.wait()
            gate = _dot(h, wg_ref[:, sl].astype(BF16))
            up = _dot(h, wu_ref[:, sl].astype(BF16))
            if 1 <= c <= FFN_NORM_GROUPS:
                up = up + zero_rows[c - 1]
            act = (gate * jax.nn.sigmoid(gate)) * up
            acc = acc + _dot(act.astype(BF16), wd_ref[sl, :].astype(BF16))
        acc_scr[...] = acc

    @pl.when(i == 0)
    def _():
        for c in range(n_chunks):
            for copy in weight_copies(c):
                copy.start()
        h_scr[...] = _rms(xprev_ref[...], g_in).astype(BF16)
        acc_scr[...] = jnp.zeros(acc_scr.shape, F32)
        step(wait_weights=True)

    @pl.when(jnp.logical_and(i > 0, i < n_tiles))
    def _():
        step(wait_weights=False)

    @pl.when(i == n_tiles)
    def _():
        finish(slice(None))


def _ffn(x, norm_g, wg, wu, wd, layer, which):
    n, d = x.shape
    n_g = norm_g.shape[1]
    n_tiles = n // FFN_ROWS
    prev_tile = lambda i: (jnp.maximum(i - 1, 0), 0)
    next_tile = lambda i: (jnp.minimum(i + 1, n_tiles - 1), 0)
    in_hbm = pl.BlockSpec(memory_space=pl.ANY)
    return pl.pallas_call(
        functools.partial(_ffn_kernel, g_row=4 * which, n_tiles=n_tiles, layer=layer, which=which),
        out_shape=jax.ShapeDtypeStruct((n, d), F32),
        grid=(n_tiles + 1,),
        in_specs=[
            pl.BlockSpec((FFN_ROWS, d), prev_tile),
            pl.BlockSpec((FFN_ROWS, d), next_tile),
            pl.BlockSpec((None, n_g, d), lambda i: (layer, 0, 0)),
            in_hbm, in_hbm, in_hbm,
        ],
        out_specs=pl.BlockSpec((FFN_ROWS, d), prev_tile),
        scratch_shapes=[pltpu.VMEM((FFN_ROWS, d), BF16), pltpu.VMEM((FFN_ROWS, d), F32),
                        pltpu.VMEM((d, D_FF), F32), pltpu.VMEM((d, D_FF), F32),
                        pltpu.VMEM((D_FF, d), F32),
                        pltpu.SemaphoreType.DMA((3, D_FF // FF_CHUNK))],
        compiler_params=_params(1),
    )(x, x, norm_g, wg, wu, wd)


def _gmlp_kernel(x_ref, g_ref, win_ref, lng_ref, lnb_ref, ws_ref, bias_ref, wout_ref,
                 o_ref, m_scr):
    x = x_ref[...]
    rows, d = x.shape
    h = _rms(x, g_ref[2:3, :]).astype(BF16)
    chunk = lambda c0: _gelu(_dot(h, win_ref[:, c0:c0 + FF_CHUNK].astype(BF16)))
    v = jnp.concatenate([chunk(d + c * FF_CHUNK) for c in range(d // FF_CHUNK)], axis=1)
    u = jnp.concatenate([chunk(c * FF_CHUNK) for c in range(d // FF_CHUNK)], axis=1)
    mu = jnp.mean(v, axis=-1, keepdims=True)
    var = jnp.mean(jnp.square(v - mu), axis=-1, keepdims=True)
    v = ((v - mu) * lax.rsqrt(var + EPS)) * lng_ref[...] + lnb_ref[...]
    ii = lax.broadcasted_iota(jnp.int32, (BLK, BLK), 0)
    jj = lax.broadcasted_iota(jnp.int32, (BLK, BLK), 1)
    causal = jj <= ii
    for grp in range(GMLP_GROUPS):
        lanes = slice(grp * BLK, (grp + 1) * BLK)
        w = jnp.where(causal, ws_ref[grp], 0.0).astype(BF16)
        for c in range(rows // BLK):
            rs = slice(c * BLK, (c + 1) * BLK)
            s = _dot(w, v[rs, lanes].astype(BF16)) + bias_ref[:, lanes]
            m_scr[rs, lanes] = (u[rs, lanes] * s).astype(BF16)
    y = _dot(m_scr[...], wout_ref[...].astype(BF16))
    o_ref[...] = x + _rms(y, g_ref[3:4, :])


def _gmlp(x, norm_g, w_in, ln_g, ln_b, w_s, bias_full, w_out, layer, j):
    n, d = x.shape
    n_g = norm_g.shape[1]
    sel = lambda i: (j, 0, 0)
    return pl.pallas_call(
        _gmlp_kernel,
        out_shape=jax.ShapeDtypeStruct((n, d), F32),
        grid=(n // GMLP_ROWS,),
        in_specs=[
            pl.BlockSpec((GMLP_ROWS, d), lambda i: (i, 0)),
            pl.BlockSpec((None, n_g, d), lambda i: (layer, 0, 0)),
            pl.BlockSpec((None, d, 2 * d), sel, pipeline_mode=pl.Buffered(1)),
            pl.BlockSpec((None, 1, d), sel),
            pl.BlockSpec((None, 1, d), sel),
            pl.BlockSpec((None, GMLP_GROUPS, BLK, BLK), lambda i: (j, 0, 0, 0)),
            pl.BlockSpec((None, BLK, d), sel),
            pl.BlockSpec((None, d, d), sel, pipeline_mode=pl.Buffered(1)),
        ],
        out_specs=pl.BlockSpec((GMLP_ROWS, d), lambda i: (i, 0)),
        scratch_shapes=[pltpu.VMEM((GMLP_ROWS, d), BF16)],
        compiler_params=_params(1),
    )(x, norm_g, w_in, ln_g, ln_b, w_s, bias_full, w_out)


def _rope(z, cos, sin, half):
    lane = lax.broadcasted_iota(jnp.int32, z.shape, 1)
    up = pltpu.roll(z, PAIR - half, axis=1)
    dn = pltpu.roll(z, half, axis=1)
    partner = jnp.where((lane & (HEAD_DIM - 1)) < half, up, dn)
    return z * cos + partner * sin


def _hyb_in_kernel(x_ref, g_ref, w_ref, ca_ref, sa_ref, cr_ref, sr_ref,
                   qa_ref, ka_ref, va_ref, q16_ref, k16_ref, v16_ref,
                   qr_ref, kr_ref, vr_ref, gr_ref, sort_scr, sort2_scr):
    h = _rms(x_ref[...], g_ref[2:3, :]).astype(BF16)
    ca, sa, cr, sr = ca_ref[...], sa_ref[...], cr_ref[...], sr_ref[...]

    def proj(group, half):
        c0 = group * MIX_W + half * 2 * PAIR
        z = _dot(h, w_ref[:, c0:c0 + 2 * PAIR].astype(BF16))
        return [(2 * half, z[:, :PAIR]), (2 * half + 1, z[:, PAIR:])]

    def sorted_rows(val):
        sort_scr[...] = val
        quarter = IN_ROWS // 4
        for r4 in range(4):
            sort2_scr[r4 * quarter:(r4 + 1) * quarter, :] = sort_scr[pl.ds(r4, quarter, stride=4), :]
        return [sort2_scr[pl.ds((r % 4) * quarter + r // 4, IN_ROWS // DIL, stride=4), :]
                for r in range(DIL)]

    for half in range(2):
        for p, z in proj(0, half):
            qa = _rope(z, ca, sa, ROPE_DIMS // 2) * QK_SCALE
            qa_ref[0, p] = qa.astype(BF16)
            for r, rows in enumerate(sorted_rows(qa)):
                q16_ref[0, p, r] = rows.astype(BF16)
        for p, z in proj(3, half):
            qr_ref[0, p] = _rope(z, cr, sr, HEAD_DIM // 2).astype(BF16)
        for p, z in proj(1, half):
            ka = _rope(z, ca, sa, ROPE_DIMS // 2)
            ka_ref[0, p] = ka.astype(BF16)
            for r, rows in enumerate(sorted_rows(ka)):
                k16_ref[0, p, r] = rows.astype(BF16)
        for p, z in proj(4, half):
            kr_ref[0, p] = _rope(z, cr, sr, HEAD_DIM // 2) * (HEAD_DIM ** -0.5)
        for p, z in proj(2, half):
            va_ref[0, p] = z.astype(BF16)
            for r, rows in enumerate(sorted_rows(z)):
                v16_ref[0, p, r] = rows.astype(BF16)
        for p, z in proj(5, half):
            vr_ref[0, p] = z.astype(BF16)
        for p, z in proj(6, half):
            gr_ref[0, p] = z


def _hyb_in(x, norm_g, w_in, tabs, layer, j, b, t):
    n, d = x.shape
    n_g = norm_g.shape[1]
    per_b = t // IN_ROWS
    nat_spec = pl.BlockSpec((1, N_PAIRS, IN_ROWS, PAIR), lambda i: (i // per_b, 0, i % per_b, 0))
    sorted_spec = pl.BlockSpec((1, N_PAIRS, DIL, IN_ROWS // DIL, PAIR),
                               lambda i: (i // per_b, 0, 0, i % per_b, 0))
    tab_spec = pl.BlockSpec((IN_ROWS, PAIR), lambda i: (i % per_b, 0))
    nat = lambda dt: jax.ShapeDtypeStruct((b, N_PAIRS, t, PAIR), dt)
    srt = jax.ShapeDtypeStruct((b, N_PAIRS, DIL, t // DIL, PAIR), BF16)
    return pl.pallas_call(
        _hyb_in_kernel,
        out_shape=[nat(BF16)] * 3 + [srt] * 3 + [nat(BF16), nat(F32), nat(BF16), nat(F32)],
        grid=(n // IN_ROWS,),
        in_specs=[
            pl.BlockSpec((IN_ROWS, d), lambda i: (i, 0)),
            pl.BlockSpec((None, n_g, d), lambda i: (layer, 0, 0)),
            pl.BlockSpec((None,) + w_in.shape[1:], lambda i: (j, 0, 0),
                         pipeline_mode=pl.Buffered(1)),
            tab_spec, tab_spec, tab_spec, tab_spec,
        ],
        out_specs=[nat_spec] * 3 + [sorted_spec] * 3 + [nat_spec] * 4,
        scratch_shapes=[pltpu.VMEM((IN_ROWS, PAIR), F32), pltpu.VMEM((IN_ROWS, PAIR), F32)],
        compiler_params=_params(1),
    )(x, norm_g, w_in, *tabs)


def _stack_heads(q):
    even = lax.broadcasted_iota(jnp.int32, q.shape, 1) < HEAD_DIM
    zero = jnp.zeros_like(q)
    return jnp.concatenate([jnp.where(even, q, zero), jnp.where(even, zero, q)], axis=0)


def _pair_select(top_bottom):
    lane = lax.broadcasted_iota(jnp.int32, (BLK, PAIR), 1)
    return jnp.where(lane < HEAD_DIM, top_bottom[:BLK], top_bottom[BLK:])


def _pair_bcast(col):
    lane = lax.broadcasted_iota(jnp.int32, (BLK, PAIR), 1)
    return jnp.where(lane < HEAD_DIM, col[:BLK], col[BLK:])


def _dil_kernel(q_ref, k_ref, v_ref, cap_ref, mult_ref, o_ref, lse_ref):
    ones = jnp.ones((4 * BLK, PAIR), BF16)
    for res, p in [(res, p) for res in range(DIL_RES) for p in range(N_PAIRS)]:
        r4 = pl.program_id(1) * DIL_RES + res
        kcat = jnp.concatenate([k_ref[0, p, ap, res] for ap in range(4)], axis=0)
        vext = jnp.concatenate(
            [jnp.concatenate([v_ref[0, p, ap, res] for ap in range(4)], axis=0), ones], axis=1)
        for a0 in range(0, 4, DIL_GROUP):
            group = range(a0, a0 + DIL_GROUP)
            trows = slice(2 * a0 * BLK, 2 * (a0 + DIL_GROUP) * BLK)
            lhs = jnp.concatenate([_stack_heads(q_ref[0, p, a, res]) for a in group], axis=0)
            s = jnp.minimum(_dot_nt(lhs, kcat), cap_ref[trows, :])
            m = jnp.max(s, axis=-1, keepdims=True)
            e = jnp.exp2(s - m)
            tile = lambda n, ap: e[2 * n * BLK:2 * (n + 1) * BLK, ap * BLK:(ap + 1) * BLK]
            pm = jnp.concatenate([
                jnp.concatenate([
                    tile(n, ap) * mult_ref[...] if ap == a else tile(n, ap)
                    for ap in range(4)], axis=1)
                for n, a in enumerate(group)], axis=0)
            acc = _dot(pm.astype(BF16), vext)
            den = acc[:, PAIR:]
            out = acc[:, :PAIR] / den
            lse = m + jnp.log(den) * LOG2_E
            for n, a in enumerate(group):
                rows = pl.ds(4 * a + r4, BLK, stride=DIL)
                o_ref[0, p, rows, :] = _pair_select(out[2 * n * BLK:2 * (n + 1) * BLK])
                lse_ref[0, p, rows, :] = _pair_select(lse[2 * n * BLK:2 * (n + 1) * BLK])


def _dil_tables():
    i = (np.arange(2 * BLK) % BLK)[:, None]
    j = np.arange(BLK)[None, :]
    cap = np.empty((4, 2 * BLK, 4, BLK), np.float32)
    for a in range(4):
        for ap in range(4):
            delta = 4 * (i - j) + (a - ap)
            in_d4 = (delta >= 0) & (delta <= BLK)
            in_d16 = (j <= i) & (a == ap)
            cap[a, :, ap, :] = np.where(in_d4 | in_d16, np.finfo(np.float32).max, NEG_INF)
    both = (i - j >= 0) & (4 * (i - j) <= BLK)
    mult = np.where(both, 2.0, 1.0).astype(np.float32)
    return jnp.asarray(cap.reshape(8 * BLK, 4 * BLK)), jnp.asarray(mult)


def _dil(q, k, v, cap, mult):
    b, _, _, nq, _ = k.shape
    t = nq * DIL
    by_res = lambda arr: arr.reshape(b, N_PAIRS, 4, 4, nq, PAIR)
    res_spec = pl.BlockSpec((1, N_PAIRS, 4, DIL_RES, nq, PAIR), lambda i, j: (i, 0, 0, j, 0, 0))
    nat_spec = pl.BlockSpec((1, N_PAIRS, t, PAIR), lambda i, j: (i, 0, 0, 0))
    nat_shape = jax.ShapeDtypeStruct((b, N_PAIRS, t, PAIR), F32)
    return pl.pallas_call(
        _dil_kernel,
        out_shape=[nat_shape, nat_shape],
        grid=(b, 4 // DIL_RES),
        in_specs=[res_spec, res_spec, res_spec,
                  pl.BlockSpec(cap.shape, lambda i, j: (0, 0)),
                  pl.BlockSpec(mult.shape, lambda i, j: (0, 0))],
        out_specs=[nat_spec, nat_spec],
        compiler_params=_params(2),
    )(by_res(q), by_res(k), by_res(v), cap, mult)


def _hyb_main_kernel(x_ref, g_ref, q_ref, k_ref, v_ref, o23_ref, l23_ref,
                     qr_ref, kr_ref, vr_ref, gr_ref,
                     cap_ref, decay_ref, zeta_ref, xi_ref, cd_ref, wout_ref,
                     o_ref, state_ref, m_scr):
    step = pl.program_id(1)

    @pl.when(step == 0)
    def _():
        state_ref[...] = jnp.zeros(state_ref.shape, F32)

    pairs = range(N_PAIRS)
    stack = lambda vals: jnp.concatenate(vals, axis=0)
    piece = lambda val, p, n: val[p * n:(p + 1) * n]
    lane = lax.broadcasted_iota(jnp.int32, (N_PAIRS * BLK, PAIR), 1)
    even = lane < HEAD_DIM
    row = lax.broadcasted_iota(jnp.int32, (N_PAIRS * BLK, PAIR), 0)
    same_head = lax.shift_right_logical(row & (PAIR - 1), 6) == lax.shift_right_logical(lane, 6)
    ones = jnp.ones((2 * BLK, PAIR), BF16)
    for c in range(MAIN_ROWS // BLK):
        rows = slice(c * BLK, (c + 1) * BLK)
        nb = step * (MAIN_ROWS // BLK) + c
        kstart = pl.multiple_of(jnp.maximum(nb - 1, 0) * BLK, BLK)
        cap = cap_ref[jnp.minimum(nb, 1)]
        s = stack([
            _dot_nt(_stack_heads(q_ref[0, p, rows, :]), k_ref[0, p, pl.ds(kstart, 2 * BLK), :])
            for p in pairs])
        s = jnp.minimum(s, cap)
        m1 = jnp.max(s, axis=-1, keepdims=True)
        e = jnp.exp2(s - m1).astype(BF16)
        acc = [_dot(piece(e, p, 2 * BLK),
                    jnp.concatenate([v_ref[0, p, pl.ds(kstart, 2 * BLK), :], ones], axis=1))
               for p in pairs]
        num1 = stack([_pair_select(acc[p][:, :PAIR]) for p in pairs])
        den1 = stack([_pair_select(acc[p][:, PAIR:]) for p in pairs])
        m1 = stack([_pair_bcast(piece(m1, p, 2 * BLK)) for p in pairs])
        l23 = stack([l23_ref[0, p, rows, :] for p in pairs])
        o23 = stack([o23_ref[0, p, rows, :] for p in pairs])
        top = jnp.maximum(l23, m1)
        w23 = jnp.exp2(l23 - top)
        w1 = jnp.exp2(m1 - top)
        attn = ((o23 * w23 + num1 * w1) / (w23 + den1 * w1)).astype(BF16)
        qst = [_stack_heads(qr_ref[0, p, rows, :]) for p in pairs]
        kr = stack([kr_ref[0, p, rows, :] for p in pairs])
        vr = [vr_ref[0, p, rows, :] for p in pairs]
        kr16 = kr.astype(BF16)
        scores = stack([_dot_nt(qst[p], piece(kr16, p, BLK)) for p in pairs]) * decay_ref[...]
        scores = scores.astype(BF16)
        inner = stack([_pair_select(_dot(piece(scores, p, 2 * BLK), vr[p])) for p in pairs])
        state = state_ref[...]
        state16 = state.astype(BF16)
        cross2 = [_dot(qst[p], piece(state16, p, PAIR)) for p in pairs]
        cross = stack([c2[:BLK] + c2[BLK:] for c2 in cross2]) * xi_ref[...]
        kz = kr * zeta_ref[...]
        kv = stack([_dot(piece(kz, p, BLK).T.astype(BF16), vr[p]) for p in pairs])
        state_ref[...] = state * cd_ref[...] + jnp.where(same_head, kv, 0.0)
        out = inner + cross
        sq = out * out
        ms_e = jnp.sum(jnp.where(even, sq, 0.0), axis=-1, keepdims=True)
        ms_o = jnp.sum(jnp.where(even, 0.0, sq), axis=-1, keepdims=True)
        ms = jnp.where(even, ms_e, ms_o) * (1.0 / HEAD_DIM)
        out = out * lax.rsqrt(ms + EPS)
        gate = stack([gr_ref[0, p, rows, :] for p in pairs])
        ret = ((gate * jax.nn.sigmoid(gate)) * out).astype(BF16)
        for p in pairs:
            m_scr[rows, p * PAIR:(p + 1) * PAIR] = piece(attn, p, BLK)
            m_scr[rows, MIX_W + p * PAIR:MIX_W + (p + 1) * PAIR] = piece(ret, p, BLK)
    y = _dot(m_scr[...], wout_ref[...].astype(BF16))
    o_ref[0] = x_ref[0] + _rms(y, g_ref[3:4, :])


def _window_cap():
    i = (np.arange(2 * BLK) % BLK)[:, None]
    j = np.arange(2 * BLK)[None, :]
    fmax = np.finfo(np.float32).max
    delta = np.stack([i - j, BLK + i - j])
    cap = np.where((delta >= 0) & (delta <= BLK), fmax, NEG_INF).astype(np.float32)
    return jnp.asarray(np.tile(cap, (1, N_PAIRS, 1)))


def _hyb_main(x, norm_g, q, k, v, o23, l23, qr, kr, vr, gr, cap, consts, w_out, layer, j):
    b, t, d = x.shape
    n_g = norm_g.shape[1]
    blk = pl.BlockSpec((1, N_PAIRS, MAIN_ROWS, PAIR), lambda i, s: (i, 0, s, 0))
    full = pl.BlockSpec((1, N_PAIRS, t, PAIR), lambda i, s: (i, 0, 0, 0))
    xblk = pl.BlockSpec((1, MAIN_ROWS, d), lambda i, s: (i, s, 0))
    const3 = lambda arr: pl.BlockSpec(arr.shape, lambda i, s: (0,) * arr.ndim)
    decay, zeta, xi, cd = consts
    return pl.pallas_call(
        _hyb_main_kernel,
        out_shape=jax.ShapeDtypeStruct((b, t, d), F32),
        grid=(b, t // MAIN_ROWS),
        in_specs=[
            xblk,
            pl.BlockSpec((None, n_g, d), lambda i, s: (layer, 0, 0)),
            blk, full, full, blk, blk,
            blk, blk, blk, blk,
            const3(cap), const3(decay), const3(zeta), const3(xi), const3(cd),
            pl.BlockSpec((None, d, d), lambda i, s: (j, 0, 0), pipeline_mode=pl.Buffered(1)),
        ],
        out_specs=xblk,
        scratch_shapes=[pltpu.VMEM((N_PAIRS * PAIR, PAIR), F32),
                        pltpu.VMEM((MAIN_ROWS, d), BF16)],
        compiler_params=_params(2),
    )(x, norm_g, q, k, v, o23, l23, qr, kr, vr, gr, cap, decay, zeta, xi, cd, w_out)


def _rope_table(t, rot_dims, theta):
    half = rot_dims // 2
    inv = theta ** (-(jnp.arange(half, dtype=F32) * 2.0 / rot_dims))
    ang = jnp.arange(t, dtype=F32)[:, None] * inv[None, :]
    cos, sin = jnp.cos(ang), jnp.sin(ang)
    rest = HEAD_DIM - rot_dims
    cos64 = jnp.concatenate([cos, cos, jnp.ones((t, rest), F32)], axis=-1)
    sin64 = jnp.concatenate([-sin, sin, jnp.zeros((t, rest), F32)], axis=-1)
    return jnp.tile(cos64, (1, 2)), jnp.tile(sin64, (1, 2))


def _retention_consts(n_heads):
    c = BLK
    log_g = jnp.log(1.0 - jnp.exp2(-5.0 - jnp.arange(n_heads, dtype=F32)))
    idx = jnp.arange(c, dtype=F32)
    diff = idx[:, None] - idx[None, :]
    decay = jnp.where(diff >= 0, jnp.exp(log_g[:, None, None] * jnp.maximum(diff, 0.0)), 0.0)
    zeta = jnp.exp(log_g[:, None] * (c - 1.0 - idx)[None, :])
    xi = jnp.exp(log_g[:, None] * (idx + 1.0)[None, :])
    chunk_decay = jnp.exp(log_g * c)
    per_lane = lambda hc: jnp.repeat(
        hc.reshape(N_PAIRS, 2, -1).transpose(0, 2, 1), HEAD_DIM, axis=-1).reshape(-1, PAIR)
    decay_st = decay.reshape(2 * N_PAIRS * c, c)
    chunk_decay = jnp.broadcast_to(chunk_decay[:, None], (n_heads, PAIR))
    return decay_st, per_lane(zeta), per_lane(xi), per_lane(chunk_decay)


def kernel(x, norm_g, ffn_w_gate, ffn_w_up, ffn_w_down, hyb_w_in, hyb_w_out,
           gmlp_w_in, gmlp_ln_g, gmlp_ln_b, gmlp_w_s, gmlp_b_s, gmlp_w_out):
    b, t, d = x.shape
    depth = norm_g.shape[0]
    wg, wu, wd = ffn_w_gate, ffn_w_up, ffn_w_down
    hyb_in, hyb_out, g_in, g_out = hyb_w_in, hyb_w_out.astype(BF16), gmlp_w_in, gmlp_w_out
    tabs = _rope_table(t, ROPE_DIMS, ROPE_THETA) + _rope_table(t, HEAD_DIM, RET_ROPE_THETA)
    consts = _retention_consts(2 * N_PAIRS)
    dil_cap, dil_mult = _dil_tables()
    win_cap = _window_cap()
    bias_full = jnp.repeat(jnp.swapaxes(gmlp_b_s, 1, 2), BLK, axis=2)
    ln_g, ln_b = gmlp_ln_g[:, None, :], gmlp_ln_b[:, None, :]
    x = x.reshape(b * t, d)
    for layer in range(depth):
        j = layer // 2
        x = _ffn(x, norm_g, wg, wu, wd, layer, 0)
        if layer % 2 == 0:
            qa, ka, va, q16, k16, v16, qr, kr, vr, gr = _hyb_in(
                x, norm_g, hyb_in, tabs, layer, j, b, t)
            o23, l23 = _dil(q16, k16, v16, dil_cap, dil_mult)
            x = _hyb_main(x.reshape(b, t, d), norm_g, qa, ka, va, o23, l23,
                          qr, kr, vr, gr, win_cap, consts, hyb_out, layer, j
                          ).reshape(b * t, d)
        else:
            x = _gmlp(x, norm_g, g_in, ln_g, ln_b, gmlp_w_s, bias_full, g_out, layer, j)
        x = _ffn(x, norm_g, wg, wu, wd, layer, 1)
    return x.reshape(b, t, d)
```

```python
import functools

import jax
import jax.numpy as jnp
import numpy as np
from jax import lax
from jax.experimental import pallas as pl
from jax.experimental.pallas import tpu as pltpu

F32 = jnp.float32
BF16 = jnp.bfloat16

D_MODEL = 1024
HEAD_DIM = 64
PAIR = 2 * HEAD_DIM
N_PAIRS = 4
MIX_W = N_PAIRS * PAIR
BLK = 128
DIL = 16
D_FF = 2816
FF_CHUNK = 256
GMLP_GROUPS = 8
ROPE_THETA = 500000.0
ROPE_DIMS = HEAD_DIM // 4
RET_ROPE_THETA = 10000.0
EPS = 1e-6
NEG_INF = -1e30
LOG2_E = np.float32(np.log2(np.e))
QK_SCALE = np.float32(HEAD_DIM ** -0.5 * np.log2(np.e))

FFN_ROWS = 512
FFN_NORM_GROUPS = 8
GMLP_ROWS = 1024
IN_ROWS = 1024
MAIN_ROWS = 1024
DIL_RES = 4
DIL_GROUP = 2
VMEM_LIMIT = 60 * 1024 * 1024


def _dot(a, b):
    return jnp.dot(a, b, preferred_element_type=F32)


def _dot_nt(a, b):
    return lax.dot_general(a, b, (((1,), (1,)), ((), ())), preferred_element_type=F32)


def _rms(x, g):
    y = x * lax.rsqrt(jnp.mean(x * x, axis=-1, keepdims=True) + EPS)
    return y * g


def _gelu(x):
    return 0.5 * x * (1.0 + lax.erf(x * np.float32(np.sqrt(0.5))))


def _params(n_axes):
    return pltpu.CompilerParams(
        dimension_semantics=("arbitrary",) * n_axes, vmem_limit_bytes=VMEM_LIMIT)


def _zero_from(*vals):
    flags = None
    for v in vals:
        for r in range(0, v.shape[0], 8):
            for l in range(0, v.shape[1], PAIR):
                f = jnp.where(v[r:r + 8, l:l + PAIR] > 0.0, 1, 0)
                flags = f if flags is None else flags | f
    zero = lax.shift_right_logical(flags, 1).astype(F32)[0:1, :]
    return jnp.concatenate([zero] * (FF_CHUNK // PAIR), axis=1)


def _ffn_kernel(xprev_ref, xnext_ref, g_ref, wg_hbm, wu_hbm, wd_hbm, o_ref,
                h_scr, acc_scr, wg_ref, wu_ref, wd_ref, sem, *, g_row, n_tiles, layer, which):
    i = pl.program_id(0)
    g_in = g_ref[g_row:g_row + 1, :]
    g_out = g_ref[g_row + 1:g_row + 2, :]
    n_chunks = D_FF // FF_CHUNK

    def weight_copies(c):
        sl = slice(c * FF_CHUNK, (c + 1) * FF_CHUNK)
        return [
            pltpu.make_async_copy(wg_hbm.at[layer, which, :, sl], wg_ref.at[:, sl], sem.at[0, c]),
            pltpu.make_async_copy(wu_hbm.at[layer, which, :, sl], wu_ref.at[:, sl], sem.at[1, c]),
            pltpu.make_async_copy(wd_hbm.at[layer, which, sl, :], wd_ref.at[sl, :], sem.at[2, c]),
        ]

    def finish(rs):
        out_prev = xprev_ref[rs, :] + 0.5 * _rms(acc_scr[rs, :], g_out)
        o_ref[rs, :] = out_prev
        return out_prev

    def step(wait_weights):
        h = h_scr[...]
        zero_rows = []
        group = FFN_ROWS // FFN_NORM_GROUPS
        for j in range(FFN_NORM_GROUPS):
            rs = slice(j * group, (j + 1) * group)
            h_next = _rms(xnext_ref[rs, :], g_in)
            h_scr[rs, :] = h_next.astype(BF16)
            zero_rows.append(_zero_from(h_next, finish(rs)))
        acts = []
        for c in range(n_chunks):
            sl = slice(c * FF_CHUNK, (c + 1) * FF_CHUNK)
            if wait_weights:
                for copy in weight_copies(c)[:2]:
                    copy.wait()
            gate = _dot(h, wg_ref[:, sl].astype(BF16))
            up = _dot(h, wu_ref[:, sl].astype(BF16))
            if 1 <= c <= FFN_NORM_GROUPS:
                up = up + zero_rows[c - 1]
            acts.append(((gate * jax.nn.sigmoid(gate)) * up).astype(BF16))
        if wait_weights:
            for c in range(n_chunks):
                weight_copies(c)[2].wait()
        acc_scr[...] = _dot(jnp.concatenate(acts, axis=1), wd_ref[...].astype(BF16))

    @pl.when(i == 0)
    def _():
        for c in range(n_chunks):
            for copy in weight_copies(c):
                copy.start()
        h_scr[...] = _rms(xprev_ref[...], g_in).astype(BF16)
        acc_scr[...] = jnp.zeros(acc_scr.shape, F32)
        step(wait_weights=True)

    @pl.when(jnp.logical_and(i > 0, i < n_tiles))
    def _():
        step(wait_weights=False)

    @pl.when(i == n_tiles)
    def _():
        finish(slice(None))


def _ffn(x, norm_g, wg, wu, wd, layer, which):
    n, d = x.shape
    n_g = norm_g.shape[1]
    n_tiles = n // FFN_ROWS
    prev_tile = lambda i: (jnp.maximum(i - 1, 0), 0)
    next_tile = lambda i: (jnp.minimum(i + 1, n_tiles - 1), 0)
    in_hbm = pl.BlockSpec(memory_space=pl.ANY)
    return pl.pallas_call(
        functools.partial(_ffn_kernel, g_row=4 * which, n_tiles=n_tiles, layer=layer, which=which),
        out_shape=jax.ShapeDtypeStruct((n, d), F32),
        grid=(n_tiles + 1,),
        in_specs=[
            pl.BlockSpec((FFN_ROWS, d), prev_tile),
            pl.BlockSpec((FFN_ROWS, d), next_tile),
            pl.BlockSpec((None, n_g, d), lambda i: (layer, 0, 0)),
            in_hbm, in_hbm, in_hbm,
        ],
        out_specs=pl.BlockSpec((FFN_ROWS, d), prev_tile),
        scratch_shapes=[pltpu.VMEM((FFN_ROWS, d), BF16), pltpu.VMEM((FFN_ROWS, d), F32),
                        pltpu.VMEM((d, D_FF), F32), pltpu.VMEM((d, D_FF), F32),
                        pltpu.VMEM((D_FF, d), F32),
                        pltpu.SemaphoreType.DMA((3, D_FF // FF_CHUNK))],
        compiler_params=_params(1),
    )(x, x, norm_g, wg, wu, wd)


def _gmlp_kernel(x_ref, g_ref, win_ref, lng_ref, lnb_ref, ws_ref, bias_ref, wout_ref,
                 o_ref, m_scr):
    x = x_ref[...]
    rows, d = x.shape
    h = _rms(x, g_ref[2:3, :]).astype(BF16)
    chunk = lambda c0: _gelu(_dot(h, win_ref[:, c0:c0 + FF_CHUNK].astype(BF16)))
    v = jnp.concatenate([chunk(d + c * FF_CHUNK) for c in range(d // FF_CHUNK)], axis=1)
    u = jnp.concatenate([chunk(c * FF_CHUNK) for c in range(d // FF_CHUNK)], axis=1)
    mu = jnp.mean(v, axis=-1, keepdims=True)
    var = jnp.mean(jnp.square(v - mu), axis=-1, keepdims=True)
    v = ((v - mu) * lax.rsqrt(var + EPS)) * lng_ref[...] + lnb_ref[...]
    ii = lax.broadcasted_iota(jnp.int32, (BLK, BLK), 0)
    jj = lax.broadcasted_iota(jnp.int32, (BLK, BLK), 1)
    causal = jj <= ii
    for grp in range(GMLP_GROUPS):
        lanes = slice(grp * BLK, (grp + 1) * BLK)
        w = jnp.where(causal, ws_ref[grp], 0.0).astype(BF16)
        for c in range(rows // BLK):
            rs = slice(c * BLK, (c + 1) * BLK)
            s = _dot(w, v[rs, lanes].astype(BF16)) + bias_ref[:, lanes]
            m_scr[rs, lanes] = (u[rs, lanes] * s).astype(BF16)
    y = _dot(m_scr[...], wout_ref[...].astype(BF16))
    o_ref[...] = x + _rms(y, g_ref[3:4, :])


def _gmlp(x, norm_g, w_in, ln_g, ln_b, w_s, bias_full, w_out, layer, j):
    n, d = x.shape
    n_g = norm_g.shape[1]
    sel = lambda i: (j, 0, 0)
    return pl.pallas_call(
        _gmlp_kernel,
        out_shape=jax.ShapeDtypeStruct((n, d), F32),
        grid=(n // GMLP_ROWS,),
        in_specs=[
            pl.BlockSpec((GMLP_ROWS, d), lambda i: (i, 0)),
            pl.BlockSpec((None, n_g, d), lambda i: (layer, 0, 0)),
            pl.BlockSpec((None, d, 2 * d), sel, pipeline_mode=pl.Buffered(1)),
            pl.BlockSpec((None, 1, d), sel),
            pl.BlockSpec((None, 1, d), sel),
            pl.BlockSpec((None, GMLP_GROUPS, BLK, BLK), lambda i: (j, 0, 0, 0)),
            pl.BlockSpec((None, BLK, d), sel),
            pl.BlockSpec((None, d, d), sel, pipeline_mode=pl.Buffered(1)),
        ],
        out_specs=pl.BlockSpec((GMLP_ROWS, d), lambda i: (i, 0)),
        scratch_shapes=[pltpu.VMEM((GMLP_ROWS, d), BF16)],
        compiler_params=_params(1),
    )(x, norm_g, w_in, ln_g, ln_b, w_s, bias_full, w_out)


def _rope(z, cos, sin, half):
    lane = lax.broadcasted_iota(jnp.int32, z.shape, 1)
    up = pltpu.roll(z, PAIR - half, axis=1)
    dn = pltpu.roll(z, half, axis=1)
    partner = jnp.where((lane & (HEAD_DIM - 1)) < half, up, dn)
    return z * cos + partner * sin


def _hyb_in_kernel(x_ref, g_ref, w_ref, ca_ref, sa_ref, cr_ref, sr_ref,
                   qa_ref, ka_ref, va_ref, q16_ref, k16_ref, v16_ref,
                   qr_ref, kr_ref, vr_ref, gr_ref, sort_scr, sort2_scr):
    h = _rms(x_ref[...], g_ref[2:3, :]).astype(BF16)
    ca, sa, cr, sr = ca_ref[...], sa_ref[...], cr_ref[...], sr_ref[...]

    def proj(group, half):
        c0 = group * MIX_W + half * 2 * PAIR
        z = _dot(h, w_ref[:, c0:c0 + 2 * PAIR].astype(BF16))
        return [(2 * half, z[:, :PAIR]), (2 * half + 1, z[:, PAIR:])]

    def sorted_rows(val):
        sort_scr[...] = val
        quarter = IN_ROWS // 4
        for r4 in range(4):
            sort2_scr[r4 * quarter:(r4 + 1) * quarter, :] = sort_scr[pl.ds(r4, quarter, stride=4), :]
        return [sort2_scr[pl.ds((r % 4) * quarter + r // 4, IN_ROWS // DIL, stride=4), :]
                for r in range(DIL)]

    for half in range(2):
        for p, z in proj(0, half):
            qa = _rope(z, ca, sa, ROPE_DIMS // 2) * QK_SCALE
            qa_ref[0, p] = qa.astype(BF16)
            for r, rows in enumerate(sorted_rows(qa)):
                q16_ref[0, p, r] = rows.astype(BF16)
        for p, z in proj(3, half):
            qr_ref[0, p] = _rope(z, cr, sr, HEAD_DIM // 2).astype(BF16)
        for p, z in proj(1, half):
            ka = _rope(z, ca, sa, ROPE_DIMS // 2)
            ka_ref[0, p] = ka.astype(BF16)
            for r, rows in enumerate(sorted_rows(ka)):
                k16_ref[0, p, r] = rows.astype(BF16)
        for p, z in proj(4, half):
            kr_ref[0, p] = _rope(z, cr, sr, HEAD_DIM // 2) * (HEAD_DIM ** -0.5)
        for p, z in proj(2, half):
            va_ref[0, p] = z.astype(BF16)
            for r, rows in enumerate(sorted_rows(z)):
                v16_ref[0, p, r] = rows.astype(BF16)
        for p, z in proj(5, half):
            vr_ref[0, p] = z.astype(BF16)
        for p, z in proj(6, half):
            gr_ref[0, p] = z


def _hyb_in(x, norm_g, w_in, tabs, layer, j, b, t):
    n, d = x.shape
    n_g = norm_g.shape[1]
    per_b = t // IN_ROWS
    nat_spec = pl.BlockSpec((1, N_PAIRS, IN_ROWS, PAIR), lambda i: (i // per_b, 0, i % per_b, 0))
    sorted_spec = pl.BlockSpec((1, N_PAIRS, DIL, IN_ROWS // DIL, PAIR),
                               lambda i: (i // per_b, 0, 0, i % per_b, 0))
    tab_spec = pl.BlockSpec((IN_ROWS, PAIR), lambda i: (i % per_b, 0))
    nat = lambda dt: jax.ShapeDtypeStruct((b, N_PAIRS, t, PAIR), dt)
    srt = jax.ShapeDtypeStruct((b, N_PAIRS, DIL, t // DIL, PAIR), BF16)
    return pl.pallas_call(
        _hyb_in_kernel,
        out_shape=[nat(BF16)] * 3 + [srt] * 3 + [nat(BF16), nat(F32), nat(BF16), nat(F32)],
        grid=(n // IN_ROWS,),
        in_specs=[
            pl.BlockSpec((IN_ROWS, d), lambda i: (i, 0)),
            pl.BlockSpec((None, n_g, d), lambda i: (layer, 0, 0)),
            pl.BlockSpec((None,) + w_in.shape[1:], lambda i: (j, 0, 0),
                         pipeline_mode=pl.Buffered(1)),
            tab_spec, tab_spec, tab_spec, tab_spec,
        ],
        out_specs=[nat_spec] * 3 + [sorted_spec] * 3 + [nat_spec] * 4,
        scratch_shapes=[pltpu.VMEM((IN_ROWS, PAIR), F32), pltpu.VMEM((IN_ROWS, PAIR), F32)],
        compiler_params=_params(1),
    )(x, norm_g, w_in, *tabs)


def _stack_heads(q):
    even = lax.broadcasted_iota(jnp.int32, q.shape, 1) < HEAD_DIM
    zero = jnp.zeros_like(q)
    return jnp.concatenate([jnp.where(even, q, zero), jnp.where(even, zero, q)], axis=0)


def _pair_select(top_bottom):
    lane = lax.broadcasted_iota(jnp.int32, (BLK, PAIR), 1)
    return jnp.where(lane < HEAD_DIM, top_bottom[:BLK], top_bottom[BLK:])


def _pair_bcast(col):
    lane = lax.broadcasted_iota(jnp.int32, (BLK, PAIR), 1)
    return jnp.where(lane < HEAD_DIM, col[:BLK], col[BLK:])


def _dil_kernel(q_ref, k_ref, v_ref, cap_ref, mult_ref, o_ref, lse_ref):
    ones = jnp.ones((4 * BLK, PAIR), BF16)
    for res, p in [(res, p) for res in range(DIL_RES) for p in range(N_PAIRS)]:
        r4 = pl.program_id(1) * DIL_RES + res
        kcat = jnp.concatenate([k_ref[0, p, ap, res] for ap in range(4)], axis=0)
        vext = jnp.concatenate(
            [jnp.concatenate([v_ref[0, p, ap, res] for ap in range(4)], axis=0), ones], axis=1)
        for a0 in range(0, 4, DIL_GROUP):
            group = range(a0, a0 + DIL_GROUP)
            trows = slice(2 * a0 * BLK, 2 * (a0 + DIL_GROUP) * BLK)
            lhs = jnp.concatenate([_stack_heads(q_ref[0, p, a, res]) for a in group], axis=0)
            s = jnp.minimum(_dot_nt(lhs, kcat), cap_ref[trows, :])
            m = jnp.max(s, axis=-1, keepdims=True)
            e = jnp.exp2(s - m)
            tile = lambda n, ap: e[2 * n * BLK:2 * (n + 1) * BLK, ap * BLK:(ap + 1) * BLK]
            pm = jnp.concatenate([
                jnp.concatenate([
                    tile(n, ap) * mult_ref[...] if ap == a else tile(n, ap)
                    for ap in range(4)], axis=1)
                for n, a in enumerate(group)], axis=0)
            acc = _dot(pm.astype(BF16), vext)
            den = acc[:, PAIR:]
            out = acc[:, :PAIR] / den
            lse = m + jnp.log(den) * LOG2_E
            for n, a in enumerate(group):
                rows = pl.ds(4 * a + r4, BLK, stride=DIL)
                o_ref[0, p, rows, :] = _pair_select(out[2 * n * BLK:2 * (n + 1) * BLK])
                lse_ref[0, p, rows, :] = _pair_select(lse[2 * n * BLK:2 * (n + 1) * BLK])


def _dil_tables():
    i = (np.arange(2 * BLK) % BLK)[:, None]
    j = np.arange(BLK)[None, :]
    cap = np.empty((4, 2 * BLK, 4, BLK), np.float32)
    for a in range(4):
        for ap in range(4):
            delta = 4 * (i - j) + (a - ap)
            in_d4 = (delta >= 0) & (delta <= BLK)
            in_d16 = (j <= i) & (a == ap)
            cap[a, :, ap, :] = np.where(in_d4 | in_d16, np.finfo(np.float32).max, NEG_INF)
    both = (i - j >= 0) & (4 * (i - j) <= BLK)
    mult = np.where(both, 2.0, 1.0).astype(np.float32)
    return jnp.asarray(cap.reshape(8 * BLK, 4 * BLK)), jnp.asarray(mult)


def _dil(q, k, v, cap, mult):
    b, _, _, nq, _ = k.shape
    t = nq * DIL
    by_res = lambda arr: arr.reshape(b, N_PAIRS, 4, 4, nq, PAIR)
    res_spec = pl.BlockSpec((1, N_PAIRS, 4, DIL_RES, nq, PAIR), lambda i, j: (i, 0, 0, j, 0, 0))
    nat_spec = pl.BlockSpec((1, N_PAIRS, t, PAIR), lambda i, j: (i, 0, 0, 0))
    nat_shape = jax.ShapeDtypeStruct((b, N_PAIRS, t, PAIR), F32)
    return pl.pallas_call(
        _dil_kernel,
        out_shape=[nat_shape, nat_shape],
        grid=(b, 4 // DIL_RES),
        in_specs=[res_spec, res_spec, res_spec,
                  pl.BlockSpec(cap.shape, lambda i, j: (0, 0)),
                  pl.BlockSpec(mult.shape, lambda i, j: (0, 0))],
        out_specs=[nat_spec, nat_spec],
        compiler_params=_params(2),
    )(by_res(q), by_res(k), by_res(v), cap, mult)


def _hyb_main_kernel(x_ref, g_ref, q_ref, k_ref, v_ref, o23_ref, l23_ref,
                     qr_ref, kr_ref, vr_ref, gr_ref,
                     cap_ref, decay_ref, zeta_ref, xi_ref, cd_ref, wout_ref,
                     o_ref, state_ref, m_scr):
    step = pl.program_id(1)

    @pl.when(step == 0)
    def _():
        state_ref[...] = jnp.zeros(state_ref.shape, F32)

    pairs = range(N_PAIRS)
    stack = lambda vals: jnp.concatenate(vals, axis=0)
    piece = lambda val, p, n: val[p * n:(p + 1) * n]
    lane = lax.broadcasted_iota(jnp.int32, (N_PAIRS * BLK, PAIR), 1)
    even = lane < HEAD_DIM
    row = lax.broadcasted_iota(jnp.int32, (N_PAIRS * BLK, PAIR), 0)
    same_head = lax.shift_right_logical(row & (PAIR - 1), 6) == lax.shift_right_logical(lane, 6)
    ones = jnp.ones((2 * BLK, PAIR), BF16)
    for c in range(MAIN_ROWS // BLK):
        rows = slice(c * BLK, (c + 1) * BLK)
        nb = step * (MAIN_ROWS // BLK) + c
        kstart = pl.multiple_of(jnp.maximum(nb - 1, 0) * BLK, BLK)
        cap = cap_ref[jnp.minimum(nb, 1)]
        s = stack([
            _dot_nt(_stack_heads(q_ref[0, p, rows, :]), k_ref[0, p, pl.ds(kstart, 2 * BLK), :])
            for p in pairs])
        s = jnp.minimum(s, cap)
        m1 = jnp.max(s, axis=-1, keepdims=True)
        e = jnp.exp2(s - m1).astype(BF16)
        acc = [_dot(piece(e, p, 2 * BLK),
                    jnp.concatenate([v_ref[0, p, pl.ds(kstart, 2 * BLK), :], ones], axis=1))
               for p in pairs]
        num1 = stack([_pair_select(acc[p][:, :PAIR]) for p in pairs])
        den1 = stack([_pair_select(acc[p][:, PAIR:]) for p in pairs])
        m1 = stack([_pair_bcast(piece(m1, p, 2 * BLK)) for p in pairs])
        l23 = stack([l23_ref[0, p, rows, :] for p in pairs])
        o23 = stack([o23_ref[0, p, rows, :] for p in pairs])
        top = jnp.maximum(l23, m1)
        w23 = jnp.exp2(l23 - top)
        w1 = jnp.exp2(m1 - top)
        attn = ((o23 * w23 + num1 * w1) / (w23 + den1 * w1)).astype(BF16)
        qst = [_stack_heads(qr_ref[0, p, rows, :]) for p in pairs]
        kr = stack([kr_ref[0, p, rows, :] for p in pairs])
        vr = [vr_ref[0, p, rows, :] for p in pairs]
        kr16 = kr.astype(BF16)
        scores = stack([_dot_nt(qst[p], piece(kr16, p, BLK)) for p in pairs]) * decay_ref[...]
        scores = scores.astype(BF16)
        inner = stack([_pair_select(_dot(piece(scores, p, 2 * BLK), vr[p])) for p in pairs])
        state = state_ref[...]
        state16 = state.astype(BF16)
        cross2 = [_dot(qst[p], piece(state16, p, PAIR)) for p in pairs]
        cross = stack([c2[:BLK] + c2[BLK:] for c2 in cross2]) * xi_ref[...]
        kz = kr * zeta_ref[...]
        kv = stack([_dot(piece(kz, p, BLK).T.astype(BF16), vr[p]) for p in pairs])
        state_ref[...] = state * cd_ref[...] + jnp.where(same_head, kv, 0.0)
        out = inner + cross
        sq = out * out
        ms_e = jnp.sum(jnp.where(even, sq, 0.0), axis=-1, keepdims=True)
        ms_o = jnp.sum(jnp.where(even, 0.0, sq), axis=-1, keepdims=True)
        ms = jnp.where(even, ms_e, ms_o) * (1.0 / HEAD_DIM)
        out = out * lax.rsqrt(ms + EPS)
        gate = stack([gr_ref[0, p, rows, :] for p in pairs])
        ret = ((gate * jax.nn.sigmoid(gate)) * out).astype(BF16)
        for p in pairs:
            m_scr[rows, p * PAIR:(p + 1) * PAIR] = piece(attn, p, BLK)
            m_scr[rows, MIX_W + p * PAIR:MIX_W + (p + 1) * PAIR] = piece(ret, p, BLK)
    y = _dot(m_scr[...], wout_ref[...].astype(BF16))
    o_ref[0] = x_ref[0] + _rms(y, g_ref[3:4, :])


def _window_cap():
    i = (np.arange(2 * BLK) % BLK)[:, None]
    j = np.arange(2 * BLK)[None, :]
    fmax = np.finfo(np.float32).max
    delta = np.stack([i - j, BLK + i - j])
    cap = np.where((delta >= 0) & (delta <= BLK), fmax, NEG_INF).astype(np.float32)
    return jnp.asarray(np.tile(cap, (1, N_PAIRS, 1)))


def _hyb_main(x, norm_g, q, k, v, o23, l23, qr, kr, vr, gr, cap, consts, w_out, layer, j):
    b, t, d = x.shape
    n_g = norm_g.shape[1]
    blk = pl.BlockSpec((1, N_PAIRS, MAIN_ROWS, PAIR), lambda i, s: (i, 0, s, 0))
    full = pl.BlockSpec((1, N_PAIRS, t, PAIR), lambda i, s: (i, 0, 0, 0))
    xblk = pl.BlockSpec((1, MAIN_ROWS, d), lambda i, s: (i, s, 0))
    const3 = lambda arr: pl.BlockSpec(arr.shape, lambda i, s: (0,) * arr.ndim)
    decay, zeta, xi, cd = consts
    return pl.pallas_call(
        _hyb_main_kernel,
        out_shape=jax.ShapeDtypeStruct((b, t, d), F32),
        grid=(b, t // MAIN_ROWS),
        in_specs=[
            xblk,
            pl.BlockSpec((None, n_g, d), lambda i, s: (layer, 0, 0)),
            blk, full, full, blk, blk,
            blk, blk, blk, blk,
            const3(cap), const3(decay), const3(zeta), const3(xi), const3(cd),
            pl.BlockSpec((None, d, d), lambda i, s: (j, 0, 0), pipeline_mode=pl.Buffered(1)),
        ],
        out_specs=xblk,
        scratch_shapes=[pltpu.VMEM((N_PAIRS * PAIR, PAIR), F32),
                        pltpu.VMEM((MAIN_ROWS, d), BF16)],
        compiler_params=_params(2),
    )(x, norm_g, q, k, v, o23, l23, qr, kr, vr, gr, cap, decay, zeta, xi, cd, w_out)


def _rope_table(t, rot_dims, theta):
    half = rot_dims // 2
    inv = theta ** (-(jnp.arange(half, dtype=F32) * 2.0 / rot_dims))
    ang = jnp.arange(t, dtype=F32)[:, None] * inv[None, :]
    cos, sin = jnp.cos(ang), jnp.sin(ang)
    rest = HEAD_DIM - rot_dims
    cos64 = jnp.concatenate([cos, cos, jnp.ones((t, rest), F32)], axis=-1)
    sin64 = jnp.concatenate([-sin, sin, jnp.zeros((t, rest), F32)], axis=-1)
    return jnp.tile(cos64, (1, 2)), jnp.tile(sin64, (1, 2))


def _retention_consts(n_heads):
    c = BLK
    log_g = jnp.log(1.0 - jnp.exp2(-5.0 - jnp.arange(n_heads, dtype=F32)))
    idx = jnp.arange(c, dtype=F32)
    diff = idx[:, None] - idx[None, :]
    decay = jnp.where(diff >= 0, jnp.exp(log_g[:, None, None] * jnp.maximum(diff, 0.0)), 0.0)
    zeta = jnp.exp(log_g[:, None] * (c - 1.0 - idx)[None, :])
    xi = jnp.exp(log_g[:, None] * (idx + 1.0)[None, :])
    chunk_decay = jnp.exp(log_g * c)
    per_lane = lambda hc: jnp.repeat(
        hc.reshape(N_PAIRS, 2, -1).transpose(0, 2, 1), HEAD_DIM, axis=-1).reshape(-1, PAIR)
    decay_st = decay.reshape(2 * N_PAIRS * c, c)
    chunk_decay = jnp.broadcast_to(chunk_decay[:, None], (n_heads, PAIR))
    return decay_st, per_lane(zeta), per_lane(xi), per_lane(chunk_decay)


def kernel(x, norm_g, ffn_w_gate, ffn_w_up, ffn_w_down, hyb_w_in, hyb_w_out,
           gmlp_w_in, gmlp_ln_g, gmlp_ln_b, gmlp_w_s, gmlp_b_s, gmlp_w_out):
    b, t, d = x.shape
    depth = norm_g.shape[0]
    wg, wu, wd = ffn_w_gate, ffn_w_up, ffn_w_down
    hyb_in, hyb_out, g_in, g_out = hyb_w_in, hyb_w_out.astype(BF16), gmlp_w_in, gmlp_w_out
    tabs = _rope_table(t, ROPE_DIMS, ROPE_THETA) + _rope_table(t, HEAD_DIM, RET_ROPE_THETA)
    consts = _retention_consts(2 * N_PAIRS)
    dil_cap, dil_mult = _dil_tables()
    win_cap = _window_cap()
    bias_full = jnp.repeat(jnp.swapaxes(gmlp_b_s, 1, 2), BLK, axis=2)
    ln_g, ln_b = gmlp_ln_g[:, None, :], gmlp_ln_b[:, None, :]
    x = x.reshape(b * t, d)
    for layer in range(depth):
        j = layer // 2
        x = _ffn(x, norm_g, wg, wu, wd, layer, 0)
        if layer % 2 == 0:
            qa, ka, va, q16, k16, v16, qr, kr, vr, gr = _hyb_in(
                x, norm_g, hyb_in, tabs, layer, j, b, t)
            o23, l23 = _dil(q16, k16, v16, dil_cap, dil_mult)
            x = _hyb_main(x.reshape(b, t, d), norm_g, qa, ka, va, o23, l23,
                          qr, kr, vr, gr, win_cap, consts, hyb_out, layer, j
                          ).reshape(b * t, d)
        else:
            x = _gmlp(x, norm_g, g_in, ln_g, ln_b, gmlp_w_s, bias_full, g_out, layer, j)
        x = _ffn(x, norm_g, wg, wu, wd, layer, 1)
    return x.reshape(b, t, d)
```
